```python
import math
import jax, jax.numpy as jnp
from jax import lax
import numpy as np

D_MODEL = 1024
BATCH = 32
SEQ = 2048
DEPTH = 1
DEC_BATCH = 16
DEC_SEQ = 2048
PAST_LEN = 128

N_MEM = 256
DA_HEADS = 4
DA_DH = D_MODEL // 16
DA_DV = 2 * DA_DH
DA_W = DA_HEADS * DA_DV
CV_W = D_MODEL // 4
CONV_K = 3
MX_HEADS = 4
MX_DH = D_MODEL // 16
MX_W = MX_HEADS * MX_DH
MIX_W = DA_W + CV_W + MX_W
QK_W = 2 * DA_HEADS * DA_DH
SPLITS = (QK_W, QK_W, DA_W, DA_W, CV_W, CV_W, CV_W, CV_W, MX_W, MX_W)
IN_W = sum(SPLITS)
ROPE_THETA = 500000.0
ROT_DIM = DA_DH // 4
Q_BLOCK = 128
ALPHA = (2 * DEPTH) ** 0.25
BETA = (8 * DEPTH) ** -0.25
LN_EPS = 1e-5

kernel_name = "hybrid_diffattn_shortconv_memxattn_encoder"


def layer_norm(x, g, b):
    x32 = x.astype(jnp.float32)
    mu = jnp.mean(x32, axis=-1, keepdims=True)
    var = jnp.mean(jnp.square(x32 - mu), axis=-1, keepdims=True)
    y = (x32 - mu) * lax.rsqrt(var + LN_EPS) * g.astype(jnp.float32) + b.astype(jnp.float32)
    return y.astype(x.dtype)


def rms_norm(x, g):
    x32 = x.astype(jnp.float32)
    y = x32 * lax.rsqrt(jnp.mean(jnp.square(x32), axis=-1, keepdims=True) + LN_EPS) * g.astype(jnp.float32)
    return y


def rope_tables(seq_len):
    inv_freq = ROPE_THETA ** (-jnp.arange(0, ROT_DIM, 2, dtype=jnp.float32) / ROT_DIM)
    ang = jnp.arange(seq_len, dtype=jnp.float32)[:, None] * inv_freq[None, :]
    return jnp.cos(ang), jnp.sin(ang)


def partial_rope(x, cos, sin):
    xr = x[..., :ROT_DIM].astype(jnp.float32)
    xp = x[..., ROT_DIM:]
    half = ROT_DIM // 2
    x1, x2 = xr[..., :half], xr[..., half:]
    c = cos[None, :, None, None, :]
    s = sin[None, :, None, None, :]
    rot = jnp.concatenate([x1 * c - x2 * s, x2 * c + x1 * s], axis=-1).astype(x.dtype)
    return jnp.concatenate([rot, xp], axis=-1)


def diff_attention(q, k, v, lam, lam_init, subln_g):
    B, S = q.shape[0], q.shape[1]
    nb = S // Q_BLOCK
    scale = DA_DH ** -0.5
    qb = jnp.moveaxis(q.reshape(B, nb, Q_BLOCK, DA_HEADS, 2, DA_DH), 1, 0)

    def block(qblk):
        s = jnp.einsum('bqhcd,bkhcd->bhcqk', qblk, k).astype(jnp.float32) * scale
        p = jax.nn.softmax(s, axis=-1)
        a = p[:, :, 0] - lam * p[:, :, 1]
        return jnp.einsum('bhqk,bkhd->bqhd', a.astype(v.dtype), v)

    o = lax.map(block, qb)
    o = jnp.moveaxis(o, 0, 1).reshape(B, S, DA_HEADS, DA_DV)
    o = rms_norm(o, subln_g) * (1.0 - lam_init)
    return o.reshape(B, S, DA_W).astype(v.dtype)


def short_conv(u, w):
    C = u.shape[-1]
    return lax.conv_general_dilated(
        u, w[:, None, :].astype(u.dtype), window_strides=(1,),
        padding=((CONV_K // 2, CONV_K // 2),),
        dimension_numbers=('NWC', 'WIO', 'NWC'), feature_group_count=C)


def memory_attention(q, mem, w_mem_kv):
    B, S = q.shape[0], q.shape[1]
    kv = mem @ w_mem_kv
    k, v = jnp.split(kv, 2, axis=-1)
    qh = q.reshape(B, S, MX_HEADS, MX_DH)
    kh = k.reshape(B, N_MEM, MX_HEADS, MX_DH)
    vh = v.reshape(B, N_MEM, MX_HEADS, MX_DH)
    s = jnp.einsum('bqhd,bkhd->bhqk', qh, kh).astype(jnp.float32) * (MX_DH ** -0.5)
    p = jax.nn.softmax(s, axis=-1).astype(v.dtype)
    o = jnp.einsum('bhqk,bkhd->bqhd', p, vh)
    return o.reshape(B, S, MX_W)


def encoder_layer(x, mem, w_in, w_mem_kv, lam_q1, lam_k1, lam_q2, lam_k2, subln_g,
                  conv_w, w_o, ln_g, ln_b, cos, sin, layer_idx):
    B, S, _ = x.shape
    proj = x @ w_in
    idx = []
    acc = 0
    for w in SPLITS[:-1]:
        acc += w
        idx.append(acc)
    q_da, k_da, v_da, g_da, b_cv, c_cv, h_cv, g_cv, q_mx, g_mx = jnp.split(proj, idx, axis=-1)

    lam_init = 0.8 - 0.6 * math.exp(-0.3 * layer_idx)
    lam = (jnp.exp(jnp.sum(lam_q1.astype(jnp.float32) * lam_k1.astype(jnp.float32)))
           - jnp.exp(jnp.sum(lam_q2.astype(jnp.float32) * lam_k2.astype(jnp.float32))) + lam_init)
    q = partial_rope(q_da.reshape(B, S, DA_HEADS, 2, DA_DH), cos, sin)
    k = partial_rope(k_da.reshape(B, S, DA_HEADS, 2, DA_DH), cos, sin)
    v = v_da.reshape(B, S, DA_HEADS, DA_DV)
    o_da = diff_attention(q, k, v, lam, lam_init, subln_g) * jax.nn.silu(g_da)

    o_cv = b_cv * short_conv(c_cv * h_cv, conv_w) * jax.nn.silu(g_cv)

    o_mx = memory_attention(q_mx, mem, w_mem_kv) * jax.nn.silu(g_mx)

    out = jnp.concatenate([o_da, o_cv, o_mx], axis=-1) @ w_o
    return layer_norm(ALPHA * x + out, ln_g, ln_b)


def trunk(x, mem, in_ln_g, in_ln_b, w_in, w_mem_kv, lam_q1, lam_k1, lam_q2, lam_k2,
          subln_g, conv_w, w_o, ln_g, ln_b):
    cos, sin = rope_tables(x.shape[1])
    h = layer_norm(x, in_ln_g, in_ln_b)
    for l in range(DEPTH):
        h = encoder_layer(h, mem, w_in[l], w_mem_kv[l], lam_q1[l], lam_k1[l], lam_q2[l], lam_k2[l],
                          subln_g[l], conv_w[l], w_o[l], ln_g[l], ln_b[l], cos, sin, l)
    return h


def setup_inputs(seed: int = 0) -> dict:
    key = jax.random.key(seed)
    ks = jax.random.split(key, 17)
    f32 = jnp.float32
    v_start = 2 * QK_W
    col_scale = jnp.ones((IN_W,), f32).at[v_start:v_start + DA_W].set(BETA)
    w_in = jax.random.normal(ks[4], (DEPTH, D_MODEL, IN_W), f32) * (D_MODEL ** -0.5) * col_scale
    kv_scale = jnp.ones((2 * MX_W,), f32).at[MX_W:].set(BETA)
    w_mem_kv = jax.random.normal(ks[5], (DEPTH, D_MODEL, 2 * MX_W), f32) * (D_MODEL ** -0.5) * kv_scale
    return {
        "x_prompt": jax.random.normal(ks[0], (BATCH, SEQ, D_MODEL), f32),
        "x_sample": jax.random.normal(ks[1], (DEC_BATCH, DEC_SEQ, D_MODEL), f32),
        "mem_prompt": jax.random.normal(ks[2], (BATCH, N_MEM, D_MODEL), f32),
        "mem_sample": jax.random.normal(ks[3], (DEC_BATCH, N_MEM, D_MODEL), f32),
        "in_ln_g": 1.0 + 0.02 * jax.random.normal(ks[6], (D_MODEL,), f32),
        "in_ln_b": 0.02 * jax.random.normal(ks[7], (D_MODEL,), f32),
        "w_in": w_in,
        "w_mem_kv": w_mem_kv,
        "lam_q1": 0.1 * jax.random.normal(ks[8], (DEPTH, DA_DH), f32),
        "lam_k1": 0.1 * jax.random.normal(ks[9], (DEPTH, DA_DH), f32),
        "lam_q2": 0.1 * jax.random.normal(ks[10], (DEPTH, DA_DH), f32),
        "lam_k2": 0.1 * jax.random.normal(ks[11], (DEPTH, DA_DH), f32),
        "subln_g": 1.0 + 0.02 * jax.random.normal(ks[12], (DEPTH, DA_DV), f32),
        "conv_w": jax.random.normal(ks[13], (DEPTH, CONV_K, CV_W), f32) * (CONV_K ** -0.5),
        "w_o": jax.random.normal(ks[14], (DEPTH, MIX_W, D_MODEL), f32) * (MIX_W ** -0.5) * BETA,
        "ln_g": 1.0 + 0.02 * jax.random.normal(ks[15], (DEPTH, D_MODEL), f32),
        "ln_b": 0.02 * jax.random.normal(ks[16], (DEPTH, D_MODEL), f32),
    }


def reference(x_prompt, x_sample, mem_prompt, mem_sample, in_ln_g, in_ln_b, w_in, w_mem_kv,
              lam_q1, lam_k1, lam_q2, lam_k2, subln_g, conv_w, w_o, ln_g, ln_b):
    y_prompt = trunk(x_prompt, mem_prompt, in_ln_g, in_ln_b, w_in, w_mem_kv, lam_q1, lam_k1,
                     lam_q2, lam_k2, subln_g, conv_w, w_o, ln_g, ln_b)
    y_sample = trunk(x_sample, mem_sample, in_ln_g, in_ln_b, w_in, w_mem_kv, lam_q1, lam_k1,
                     lam_q2, lam_k2, subln_g, conv_w, w_o, ln_g, ln_b)
    return (y_prompt, y_sample)
```

```python
import functools
import math

import jax
import jax.numpy as jnp
from jax import lax
from jax.experimental import pallas as pl
from jax.experimental.pallas import tpu as pltpu

LANE = 128
DA_HEADS = 4
DA_DH = 64
DA_DV = 128
MX_HEADS = 4
MX_DH = 64
ROT_DIM = 16
ROPE_THETA = 500000.0
LN_EPS = 1e-5
LAM_INIT = 0.8 - 0.6 * math.exp(-0.3 * 0)
ALPHA = 2.0 ** 0.25

PROJ_ROWS = 512
ATTN_ROWS = 256
OUT_ROWS = 512
VMEM_LIMIT = 56 * 1024 * 1024

f32 = jnp.float32
bf16 = jnp.bfloat16


def _layer_norm(x, g, b):
    mu = jnp.mean(x, axis=-1, keepdims=True)
    xc = x - mu
    var = jnp.mean(xc * xc, axis=-1, keepdims=True)
    return xc * lax.rsqrt(var + LN_EPS) * g + b


def _silu(x):
    return x / (1.0 + jnp.exp(-x))


def _in_proj_kernel(x_ref, g_ref, b_ref, w_ref, cos_ref, sinp_ref, sinm_ref,
                    q_ref, k_ref, v_ref, sg_ref, u_ref, bg_ref, qm_ref, sgm_ref):
    t = pl.program_id(1)
    rows = x_ref.shape[1]
    h = _layer_norm(x_ref[0], g_ref[...], b_ref[...]).astype(bf16)

    r0 = pl.multiple_of(t * rows, rows)
    cos = cos_ref[pl.ds(r0, rows), :]
    sinp = sinp_ref[pl.ds(r0, rows), :]
    sinm = sinm_ref[pl.ds(r0, rows), :]

    def rope(c):
        return c * cos + pltpu.roll(c, 8, 1) * sinp + pltpu.roll(c, LANE - 8, 1) * sinm

    qk = jnp.dot(h, w_ref[:, 0:1024], preferred_element_type=f32)
    for j in range(DA_HEADS):
        q_ref[0, j] = (rope(qk[:, j * LANE:(j + 1) * LANE]) * (DA_DH ** -0.5)).astype(bf16)
        k_ref[0, j] = rope(qk[:, 512 + j * LANE:512 + (j + 1) * LANE]).astype(bf16)

    vg = jnp.dot(h, w_ref[:, 1024:2048], preferred_element_type=f32)
    for j in range(DA_HEADS):
        v_ref[0, j] = vg[:, j * LANE:(j + 1) * LANE].astype(bf16)
    sg_ref[0] = _silu(vg[:, 512:1024]).astype(bf16)

    cv = jnp.dot(h, w_ref[:, 2048:3072], preferred_element_type=f32)
    u_ref[0] = cv[:, 256:512] * cv[:, 512:768]
    bg_ref[0] = (cv[:, 0:256] * _silu(cv[:, 768:1024])).astype(bf16)

    mx = jnp.dot(h, w_ref[:, 3072:3584], preferred_element_type=f32)
    qm_ref[0] = (mx[:, 0:256] * (MX_DH ** -0.5)).astype(bf16)
    sgm_ref[0] = _silu(mx[:, 256:512]).astype(bf16)


def _in_proj(x, in_g, in_b, w_in, cos, sinp, sinm):
    B, S, D = x.shape
    rows = PROJ_ROWS
    grid = (B, S // rows)
    const = lambda b, t: (0, 0)
    head_spec = pl.BlockSpec((1, DA_HEADS, rows, LANE), lambda b, t: (b, 0, t, 0))
    row_spec = lambda w: pl.BlockSpec((1, rows, w), lambda b, t: (b, t, 0))
    head_shape = jax.ShapeDtypeStruct((B, DA_HEADS, S, LANE), bf16)
    return pl.pallas_call(
        _in_proj_kernel,
        grid=grid,
        in_specs=[
            pl.BlockSpec((1, rows, D), lambda b, t: (b, t, 0)),
            pl.BlockSpec((1, D), const),
            pl.BlockSpec((1, D), const),
            pl.BlockSpec(w_in.shape, const),
            pl.BlockSpec((S, LANE), const),
            pl.BlockSpec((S, LANE), const),
            pl.BlockSpec((S, LANE), const),
        ],
        out_specs=[head_spec, head_spec, head_spec, row_spec(512), row_spec(256),
                   row_spec(256), row_spec(256), row_spec(256)],
        out_shape=[head_shape, head_shape, head_shape,
                   jax.ShapeDtypeStruct((B, S, 512), bf16),
                   jax.ShapeDtypeStruct((B, S, 256), f32),
                   jax.ShapeDtypeStruct((B, S, 256), bf16),
                   jax.ShapeDtypeStruct((B, S, 256), bf16),
                   jax.ShapeDtypeStruct((B, S, 256), bf16)],
        compiler_params=pltpu.CompilerParams(
            dimension_semantics=("parallel", "arbitrary"),
            vmem_limit_bytes=VMEM_LIMIT),
        name="in_proj",
    )(x, in_g, in_b, w_in, cos, sinp, sinm)


def _diff_attn_kernel(q_ref, k_ref, v_ref, sg_ref, lq1_ref, lk1_ref, lq2_ref, lk2_ref, sub_ref,
                      o_ref, vext_ref):
    S = k_ref.shape[2]
    tq = ATTN_ROWS
    vext_ref[:, 0:DA_DV] = v_ref[0, 0]
    vext_ref[:, DA_DV:2 * DA_DV] = jnp.ones((S, DA_DV), bf16)

    lam = (jnp.exp(jnp.sum(lq1_ref[...] * lk1_ref[...], axis=-1, keepdims=True))
           - jnp.exp(jnp.sum(lq2_ref[...] * lk2_ref[...], axis=-1, keepdims=True)) + LAM_INIT)
    gain = sub_ref[...] * (1.0 - LAM_INIT)
    lane = lax.broadcasted_iota(jnp.int32, (tq, LANE), 1)
    first_map = lane < DA_DH

    def step(i, carry):
        r0 = pl.multiple_of(i * tq, tq)
        qh = q_ref[0, 0, pl.ds(r0, tq), :]
        zero = jnp.zeros_like(qh)
        lhs = jnp.concatenate([jnp.where(first_map, qh, zero), jnp.where(first_map, zero, qh)], axis=0)
        s = lax.dot_general(lhs, k_ref[0, 0], (((1,), (1,)), ((), ())), preferred_element_type=f32)
        m = jnp.max(s, axis=-1, keepdims=True)
        p = jnp.exp(s - m).astype(bf16)
        pv = jnp.dot(p, vext_ref[...], preferred_element_type=f32)
        o = pv[:tq, :DA_DV] / pv[:tq, DA_DV:] - lam * (pv[tq:, :DA_DV] / pv[tq:, DA_DV:])
        o = o * lax.rsqrt(jnp.mean(o * o, axis=-1, keepdims=True) + LN_EPS) * gain
        o_ref[0, pl.ds(r0, tq), :] = (o * sg_ref[0, pl.ds(r0, tq), :].astype(f32)).astype(bf16)
        return carry

    lax.fori_loop(0, S // tq, step, 0)


def _diff_attn(q, k, v, sg, lq1, lk1, lq2, lk2, subln_g):
    B, H, S, _ = q.shape
    head_spec = pl.BlockSpec((1, 1, S, LANE), lambda b, h: (b, h, 0, 0))
    col_spec = pl.BlockSpec((1, S, LANE), lambda b, h: (b, 0, h))
    const = lambda b, h: (0, 0)
    return pl.pallas_call(
        _diff_attn_kernel,
        grid=(B, H),
        in_specs=[head_spec, head_spec, head_spec, col_spec,
                  pl.BlockSpec((1, DA_DH), const), pl.BlockSpec((1, DA_DH), const),
                  pl.BlockSpec((1, DA_DH), const), pl.BlockSpec((1, DA_DH), const),
                  pl.BlockSpec((1, DA_DV), const)],
        out_specs=col_spec,
        out_shape=jax.ShapeDtypeStruct((B, S, H * DA_DV), bf16),
        scratch_shapes=[pltpu.VMEM((S, 2 * DA_DV), bf16)],
        compiler_params=pltpu.CompilerParams(
            dimension_semantics=("parallel", "arbitrary"),
            vmem_limit_bytes=VMEM_LIMIT),
        name="diff_attn",
    )(q, k, v, sg, lq1, lk1, lq2, lk2, subln_g)


def _mix_out_kernel(x_ref, oda_ref, bg_ref, qm_ref, sgm_ref, u_ref, mem_ref, wkv_ref, cw_ref, wo_ref,
                    ing_ref, inb_ref, lng_ref, lnb_ref, y_ref, memk_ref, memv_ref, conv_ref):
    t = pl.program_id(1)
    rows = x_ref.shape[1]
    S = u_ref.shape[1]
    mxw = MX_HEADS * MX_DH

    @pl.when(t == 0)
    def _():
        kv = jnp.dot(mem_ref[0].astype(bf16), wkv_ref[...], preferred_element_type=f32)
        memk_ref[...] = kv[:, :mxw].astype(bf16)
        memv_ref[...] = kv[:, mxw:].astype(bf16)
        u = u_ref[0]
        row = lax.broadcasted_iota(jnp.int32, u.shape, 0)
        prev = jnp.where(row == 0, 0.0, pltpu.roll(u, 1, 0))
        nxt = jnp.where(row == S - 1, 0.0, pltpu.roll(u, S - 1, 0))
        conv_ref[...] = prev * cw_ref[0:1, :] + u * cw_ref[1:2, :] + nxt * cw_ref[2:3, :]

    r0 = pl.multiple_of(t * rows, rows)
    o_cv = (bg_ref[0].astype(f32) * conv_ref[pl.ds(r0, rows), :]).astype(bf16)

    qm = qm_ref[0]
    lane = lax.broadcasted_iota(jnp.int32, (rows, mxw), 1)
    zero = jnp.zeros_like(qm)
    head_of_lane = lane // MX_DH
    lhs = jnp.concatenate([jnp.where(head_of_lane == j, qm, zero) for j in range(MX_HEADS)], axis=0)
    s = lax.dot_general(lhs, memk_ref[...], (((1,), (1,)), ((), ())), preferred_element_type=f32)
    m = jnp.max(s, axis=-1, keepdims=True)
    p = jnp.exp(s - m)
    l = jnp.sum(p, axis=-1, keepdims=True)
    r = jnp.dot(p.astype(bf16), memv_ref[...], preferred_element_type=f32) / l
    o_mx = jnp.zeros((rows, mxw), f32)
    for j in range(MX_HEADS):
        o_mx = jnp.where(head_of_lane == j, r[j * rows:(j + 1) * rows], o_mx)
    o_mx = (o_mx * sgm_ref[0].astype(f32)).astype(bf16)

    out = (jnp.dot(oda_ref[0], wo_ref[0:512, :], preferred_element_type=f32)
           + jnp.dot(o_cv, wo_ref[512:768, :], preferred_element_type=f32)
           + jnp.dot(o_mx, wo_ref[768:1024, :], preferred_element_type=f32))
    h = _layer_norm(x_ref[0], ing_ref[...], inb_ref[...])
    y_ref[0] = _layer_norm(ALPHA * h + out, lng_ref[...], lnb_ref[...])


def _mix_out(x, oda, bg, qm, sgm, u, mem, wkv, conv_w, wo, in_g, in_b, ln_g, ln_b):
    B, S, D = x.shape
    rows = OUT_ROWS
    n_mem = mem.shape[1]
    const = lambda b, t: (0, 0)
    row_spec = lambda w: pl.BlockSpec((1, rows, w), lambda b, t: (b, t, 0))
    return pl.pallas_call(
        _mix_out_kernel,
        grid=(B, S // rows),
        in_specs=[row_spec(D), row_spec(512), row_spec(256), row_spec(256), row_spec(256),
                  pl.BlockSpec((1, S, 256), lambda b, t: (b, 0, 0)),
                  pl.BlockSpec((1, n_mem, D), lambda b, t: (b, 0, 0)),
                  pl.BlockSpec(wkv.shape, const),
                  pl.BlockSpec(conv_w.shape, const),
                  pl.BlockSpec(wo.shape, const),
                  pl.BlockSpec((1, D), const), pl.BlockSpec((1, D), const),
                  pl.BlockSpec((1, D), const), pl.BlockSpec((1, D), const)],
        out_specs=row_spec(D),
        out_shape=jax.ShapeDtypeStruct((B, S, D), f32),
        scratch_shapes=[pltpu.VMEM((n_mem, MX_HEADS * MX_DH), bf16),
                        pltpu.VMEM((n_mem, MX_HEADS * MX_DH), bf16),
                        pltpu.VMEM((S, 256), f32)],
        compiler_params=pltpu.CompilerParams(
            dimension_semantics=("parallel", "arbitrary"),
            vmem_limit_bytes=VMEM_LIMIT),
        name="mix_out",
    )(x, oda, bg, qm, sgm, u, mem, wkv, conv_w, wo, in_g, in_b, ln_g, ln_b)


def _rope_tables(seq_len):
    inv_freq = ROPE_THETA ** (-jnp.arange(0, ROT_DIM, 2, dtype=f32) / ROT_DIM)
    ang = jnp.arange(seq_len, dtype=f32)[:, None] * inv_freq[None, :]
    cos, sin = jnp.cos(ang), jnp.sin(ang)
    half = ROT_DIM // 2
    pad = DA_DH - ROT_DIM
    ones = jnp.ones((seq_len, pad), f32)
    zeros_h = jnp.zeros((seq_len, half), f32)
    zeros_p = jnp.zeros((seq_len, pad), f32)
    c64 = jnp.concatenate([cos, cos, ones], axis=-1)
    sp64 = jnp.concatenate([zeros_h, sin, zeros_p], axis=-1)
    sm64 = jnp.concatenate([-sin, zeros_h, zeros_p], axis=-1)
    tile = lambda a: jnp.concatenate([a, a], axis=-1)
    return tile(c64), tile(sp64), tile(sm64)


def _trunk(x, mem, in_g, in_b, w_in, wkv, lq1, lk1, lq2, lk2, subln_g, conv_w, wo, ln_g, ln_b, tables):
    q, k, v, sg, u, bg, qm, sgm = _in_proj(x, in_g, in_b, w_in, *tables)
    oda = _diff_attn(q, k, v, sg, lq1, lk1, lq2, lk2, subln_g)
    return _mix_out(x, oda, bg, qm, sgm, u, mem, wkv, conv_w, wo, in_g, in_b, ln_g, ln_b)


def kernel(x_prompt, x_sample, mem_prompt, mem_sample, in_ln_g, in_ln_b, w_in, w_mem_kv, lam_q1, lam_k1, lam_q2, lam_k2, subln_g, conv_w, w_o, ln_g, ln_b):
    assert w_in.shape[0] == 1, "single-layer trunk"
    assert x_prompt.shape[1] == x_sample.shape[1]
    tables = _rope_tables(x_prompt.shape[1])
    params = (in_ln_g[None, :], in_ln_b[None, :], w_in[0].astype(bf16), w_mem_kv[0].astype(bf16),
              lam_q1, lam_k1, lam_q2, lam_k2, subln_g, conv_w[0], w_o[0].astype(bf16), ln_g, ln_b)
    y_prompt = _trunk(x_prompt, mem_prompt, *params, tables)
    y_sample = _trunk(x_sample, mem_sample, *params, tables)
    return (y_prompt, y_sample)
```

```python
import math

import jax
import jax.numpy as jnp
from jax import lax
from jax.experimental import pallas as pl
from jax.experimental.pallas import tpu as pltpu

LANE = 128
DA_HEADS = 4
DA_DH = 64
DA_DV = 128
MX_HEADS = 4
MX_DH = 64
ROT_DIM = 16
ROPE_THETA = 500000.0
LN_EPS = 1e-5
LAM_INIT = 0.8 - 0.6 * math.exp(-0.3 * 0)
ALPHA = 2.0 ** 0.25

PROJ_ROWS = 512
ATTN_ROWS = 512
FINISH_ROWS = 32
KEY_TILE = 256
OUT_ROWS = 512
VMEM_LIMIT = 56 * 1024 * 1024

f32 = jnp.float32
bf16 = jnp.bfloat16


def _layer_norm(x, g, b):
    mu = jnp.mean(x, axis=-1, keepdims=True)
    xc = x - mu
    var = jnp.mean(xc * xc, axis=-1, keepdims=True)
    return xc * lax.rsqrt(var + LN_EPS) * g + b


def _silu(x):
    return x / (1.0 + jnp.exp(-x))


def _in_proj_kernel(x_ref, g_ref, b_ref, w_ref, cos_ref, sinp_ref, sinm_ref,
                    q_ref, k_ref, v_ref, sg_ref, u_ref, bg_ref, qm_ref, sgm_ref):
    t = pl.program_id(1)
    rows = x_ref.shape[1]
    h = _layer_norm(x_ref[0], g_ref[...], b_ref[...]).astype(bf16)

    r0 = pl.multiple_of(t * rows, rows)
    cos = cos_ref[pl.ds(r0, rows), :]
    sinp = sinp_ref[pl.ds(r0, rows), :]
    sinm = sinm_ref[pl.ds(r0, rows), :]

    def rope(c):
        return c * cos + pltpu.roll(c, 8, 1) * sinp + pltpu.roll(c, LANE - 8, 1) * sinm

    qk = jnp.dot(h, w_ref[:, 0:1024], preferred_element_type=f32)
    for j in range(DA_HEADS):
        q_ref[0, j] = (rope(qk[:, j * LANE:(j + 1) * LANE]) * (DA_DH ** -0.5)).astype(bf16)
        k_ref[0, j] = rope(qk[:, 512 + j * LANE:512 + (j + 1) * LANE]).astype(bf16)

    vg = jnp.dot(h, w_ref[:, 1024:2048], preferred_element_type=f32)
    for j in range(DA_HEADS):
        v_ref[0, j] = vg[:, j * LANE:(j + 1) * LANE].astype(bf16)
        sg_ref[0, j] = _silu(vg[:, 512 + j * LANE:512 + (j + 1) * LANE]).astype(bf16)

    cv = jnp.dot(h, w_ref[:, 2048:3072], preferred_element_type=f32)
    u_ref[0] = cv[:, 256:512] * cv[:, 512:768]
    bg_ref[0] = (cv[:, 0:256] * _silu(cv[:, 768:1024])).astype(bf16)

    mx = jnp.dot(h, w_ref[:, 3072:3584], preferred_element_type=f32)
    qm_ref[0] = (mx[:, 0:256] * (MX_DH ** -0.5)).astype(bf16)
    sgm_ref[0] = _silu(mx[:, 256:512]).astype(bf16)


def _in_proj(x, in_g, in_b, w_in, cos, sinp, sinm):
    B, S, D = x.shape
    rows = PROJ_ROWS
    grid = (B, S // rows)
    const = lambda b, t: (0, 0)
    head_spec = pl.BlockSpec((1, DA_HEADS, rows, LANE), lambda b, t: (b, 0, t, 0))
    row_spec = lambda w: pl.BlockSpec((1, rows, w), lambda b, t: (b, t, 0))
    head_shape = jax.ShapeDtypeStruct((B, DA_HEADS, S, LANE), bf16)
    return pl.pallas_call(
        _in_proj_kernel,
        grid=grid,
        in_specs=[
            pl.BlockSpec((1, rows, D), lambda b, t: (b, t, 0)),
            pl.BlockSpec((1, D), const),
            pl.BlockSpec((1, D), const),
            pl.BlockSpec(w_in.shape, const),
            pl.BlockSpec((S, LANE), const),
            pl.BlockSpec((S, LANE), const),
            pl.BlockSpec((S, LANE), const),
        ],
        out_specs=[head_spec, head_spec, head_spec, head_spec, row_spec(256),
                   row_spec(256), row_spec(256), row_spec(256)],
        out_shape=[head_shape, head_shape, head_shape, head_shape,
                   jax.ShapeDtypeStruct((B, S, 256), f32),
                   jax.ShapeDtypeStruct((B, S, 256), bf16),
                   jax.ShapeDtypeStruct((B, S, 256), bf16),
                   jax.ShapeDtypeStruct((B, S, 256), bf16)],
        compiler_params=pltpu.CompilerParams(
            dimension_semantics=("parallel", "arbitrary"),
            vmem_limit_bytes=VMEM_LIMIT),
        name="in_proj",
    )(x, in_g, in_b, w_in, cos, sinp, sinm)


def _diff_attn_kernel(q_ref, k_ref, v_ref, sg_ref, lq1_ref, lk1_ref, lq2_ref, lk2_ref, sub_ref,
                      o_ref, vext_ref, s0_ref, s1_ref, m0_ref, m1_ref):
    H, S = k_ref.shape[1], k_ref.shape[2]
    tq = ATTN_ROWS
    tiles_per_head = S // tq
    n_tiles = H * tiles_per_head
    for h in range(H):
        vext_ref[h, :, 0:DA_DV] = v_ref[0, h]
        vext_ref[h, :, DA_DV:2 * DA_DV] = jnp.ones((S, DA_DV), bf16)

    lam = (jnp.exp(jnp.sum(lq1_ref[...] * lk1_ref[...], axis=-1, keepdims=True))
           - jnp.exp(jnp.sum(lq2_ref[...] * lk2_ref[...], axis=-1, keepdims=True)) + LAM_INIT)
    gain = sub_ref[...] * (1.0 - LAM_INIT)
    first_map = lax.broadcasted_iota(jnp.int32, (tq, LANE), 1) < DA_DH

    def tile_pos(n):
        return n // tiles_per_head, pl.multiple_of((n % tiles_per_head) * tq, tq)

    key_tiles = S // KEY_TILE

    def scores_and_row_max(n, s_ref, m_ref):
        h, r0 = tile_pos(n)
        qh = q_ref[0, h, pl.ds(r0, tq), :]
        zero = jnp.zeros_like(qh)
        lhs = jnp.concatenate([jnp.where(first_map, qh, zero), jnp.where(first_map, zero, qh)], axis=0)
        m = None
        for j in range(key_tiles):
            keys = pl.ds(j * KEY_TILE, KEY_TILE)
            s = lax.dot_general(lhs, k_ref[0, h, keys, :], (((1,), (1,)), ((), ())), preferred_element_type=f32)
            s_ref[:, keys] = s
            mj = jnp.max(s, axis=-1, keepdims=True)
            m = mj if m is None else jnp.maximum(m, mj)
        m_ref[...] = jnp.broadcast_to(m, (2 * tq, LANE))

    def weighted_values(n, s_ref, m_ref):
        h, r0 = tile_pos(n)
        pv = None
        for j in range(key_tiles):
            halves = [jnp.exp(s_ref[:, pl.ds(j * KEY_TILE + i * LANE, LANE)] - m_ref[...]).astype(bf16)
                      for i in range(KEY_TILE // LANE)]
            part = jnp.dot(jnp.concatenate(halves, axis=1), vext_ref[h, pl.ds(j * KEY_TILE, KEY_TILE), :],
                           preferred_element_type=f32)
            pv = part if pv is None else pv + part
        for c in range(tq // FINISH_ROWS):
            a, b = c * FINISH_ROWS, (c + 1) * FINISH_ROWS
            o = pv[a:b, :DA_DV] / pv[a:b, DA_DV:] - lam * (pv[tq + a:tq + b, :DA_DV] / pv[tq + a:tq + b, DA_DV:])
            o = o * lax.rsqrt(jnp.mean(o * o, axis=-1, keepdims=True) + LN_EPS) * gain
            rows = pl.ds(r0 + a, FINISH_ROWS)
            o_ref[0, h, rows, :] = (o * sg_ref[0, h, rows, :].astype(f32)).astype(bf16)

    scores_and_row_max(0, s0_ref, m0_ref)

    def steady(n, carry):
        @pl.when(n % 2 == 1)
        def _():
            scores_and_row_max(n, s1_ref, m1_ref)
            weighted_values(n - 1, s0_ref, m0_ref)

        @pl.when(n % 2 == 0)
        def _():
            scores_and_row_max(n, s0_ref, m0_ref)
            weighted_values(n - 1, s1_ref, m1_ref)

        return carry

    lax.fori_loop(1, n_tiles, steady, 0)
    weighted_values(n_tiles - 1, s1_ref, m1_ref)


def _diff_attn(q, k, v, sg, lq1, lk1, lq2, lk2, subln_g):
    B, H, S, _ = q.shape
    head_spec = pl.BlockSpec((1, H, S, LANE), lambda b: (b, 0, 0, 0))
    const = lambda b: (0, 0)
    return pl.pallas_call(
        _diff_attn_kernel,
        grid=(B,),
        in_specs=[head_spec, head_spec, head_spec, head_spec,
                  pl.BlockSpec((1, DA_DH), const), pl.BlockSpec((1, DA_DH), const),
                  pl.BlockSpec((1, DA_DH), const), pl.BlockSpec((1, DA_DH), const),
                  pl.BlockSpec((1, DA_DV), const)],
        out_specs=head_spec,
        out_shape=jax.ShapeDtypeStruct((B, H, S, DA_DV), bf16),
        scratch_shapes=[pltpu.VMEM((H, S, 2 * DA_DV), bf16),
                        pltpu.VMEM((2 * ATTN_ROWS, S), f32), pltpu.VMEM((2 * ATTN_ROWS, S), f32),
                        pltpu.VMEM((2 * ATTN_ROWS, LANE), f32), pltpu.VMEM((2 * ATTN_ROWS, LANE), f32)],
        compiler_params=pltpu.CompilerParams(
            dimension_semantics=("parallel",),
            vmem_limit_bytes=VMEM_LIMIT),
        name="diff_attn",
    )(q, k, v, sg, lq1, lk1, lq2, lk2, subln_g)


def _mix_out_kernel(x_ref, oda_ref, bg_ref, qm_ref, sgm_ref, u_ref, mem_ref, wkv_ref, cw_ref, wo_ref,
                    ing_ref, inb_ref, lng_ref, lnb_ref, y_ref, memk_ref, memv_ref, conv_ref):
    t = pl.program_id(1)
    rows = x_ref.shape[1]
    S = u_ref.shape[1]
    mxw = MX_HEADS * MX_DH

    @pl.when(t == 0)
    def _():
        kv = jnp.dot(mem_ref[0].astype(bf16), wkv_ref[...], preferred_element_type=f32)
        memk_ref[...] = kv[:, :mxw].astype(bf16)
        memv_ref[...] = kv[:, mxw:].astype(bf16)
        u = u_ref[0]
        row = lax.broadcasted_iota(jnp.int32, u.shape, 0)
        prev = jnp.where(row == 0, 0.0, pltpu.roll(u, 1, 0))
        nxt = jnp.where(row == S - 1, 0.0, pltpu.roll(u, S - 1, 0))
        conv_ref[...] = prev * cw_ref[0:1, :] + u * cw_ref[1:2, :] + nxt * cw_ref[2:3, :]

    r0 = pl.multiple_of(t * rows, rows)
    o_cv = (bg_ref[0].astype(f32) * conv_ref[pl.ds(r0, rows), :]).astype(bf16)

    qm = qm_ref[0]
    lane = lax.broadcasted_iota(jnp.int32, (rows, mxw), 1)
    zero = jnp.zeros_like(qm)
    head_of_lane = lane // MX_DH
    lhs = jnp.concatenate([jnp.where(head_of_lane == j, qm, zero) for j in range(MX_HEADS)], axis=0)
    s = lax.dot_general(lhs, memk_ref[...], (((1,), (1,)), ((), ())), preferred_element_type=f32)
    m = jnp.max(s, axis=-1, keepdims=True)
    p = jnp.exp(s - m)
    l = jnp.sum(p, axis=-1, keepdims=True)
    r = jnp.dot(p.astype(bf16), memv_ref[...], preferred_element_type=f32) / l
    o_mx = jnp.zeros((rows, mxw), f32)
    for j in range(MX_HEADS):
        o_mx = jnp.where(head_of_lane == j, r[j * rows:(j + 1) * rows], o_mx)
    o_mx = (o_mx * sgm_ref[0].astype(f32)).astype(bf16)

    oda = jnp.concatenate([oda_ref[0, j] for j in range(DA_HEADS)], axis=-1)
    out = (jnp.dot(oda, wo_ref[0:512, :], preferred_element_type=f32)
           + jnp.dot(o_cv, wo_ref[512:768, :], preferred_element_type=f32)
           + jnp.dot(o_mx, wo_ref[768:1024, :], preferred_element_type=f32))
    h = _layer_norm(x_ref[0], ing_ref[...], inb_ref[...])
    y_ref[0] = _layer_norm(ALPHA * h + out, lng_ref[...], lnb_ref[...])


def _mix_out(x, oda, bg, qm, sgm, u, mem, wkv, conv_w, wo, in_g, in_b, ln_g, ln_b):
    B, S, D = x.shape
    rows = OUT_ROWS
    n_mem = mem.shape[1]
    const = lambda b, t: (0, 0)
    row_spec = lambda w: pl.BlockSpec((1, rows, w), lambda b, t: (b, t, 0))
    return pl.pallas_call(
        _mix_out_kernel,
        grid=(B, S // rows),
        in_specs=[row_spec(D), pl.BlockSpec((1, DA_HEADS, rows, LANE), lambda b, t: (b, 0, t, 0)),
                  row_spec(256), row_spec(256), row_spec(256),
                  pl.BlockSpec((1, S, 256), lambda b, t: (b, 0, 0)),
                  pl.BlockSpec((1, n_mem, D), lambda b, t: (b, 0, 0)),
                  pl.BlockSpec(wkv.shape, const),
                  pl.BlockSpec(conv_w.shape, const),
                  pl.BlockSpec(wo.shape, const),
                  pl.BlockSpec((1, D), const), pl.BlockSpec((1, D), const),
                  pl.BlockSpec((1, D), const), pl.BlockSpec((1, D), const)],
        out_specs=row_spec(D),
        out_shape=jax.ShapeDtypeStruct((B, S, D), f32),
        scratch_shapes=[pltpu.VMEM((n_mem, MX_HEADS * MX_DH), bf16),
                        pltpu.VMEM((n_mem, MX_HEADS * MX_DH), bf16),
                        pltpu.VMEM((S, 256), f32)],
        compiler_params=pltpu.CompilerParams(
            dimension_semantics=("parallel", "arbitrary"),
            vmem_limit_bytes=VMEM_LIMIT),
        name="mix_out",
    )(x, oda, bg, qm, sgm, u, mem, wkv, conv_w, wo, in_g, in_b, ln_g, ln_b)


def _rope_tables(seq_len):
    inv_freq = ROPE_THETA ** (-jnp.arange(0, ROT_DIM, 2, dtype=f32) / ROT_DIM)
    ang = jnp.arange(seq_len, dtype=f32)[:, None] * inv_freq[None, :]
    cos, sin = jnp.cos(ang), jnp.sin(ang)
    half = ROT_DIM // 2
    pad = DA_DH - ROT_DIM
    ones = jnp.ones((seq_len, pad), f32)
    zeros_h = jnp.zeros((seq_len, half), f32)
    zeros_p = jnp.zeros((seq_len, pad), f32)
    c64 = jnp.concatenate([cos, cos, ones], axis=-1)
    sp64 = jnp.concatenate([zeros_h, sin, zeros_p], axis=-1)
    sm64 = jnp.concatenate([-sin, zeros_h, zeros_p], axis=-1)
    tile = lambda a: jnp.concatenate([a, a], axis=-1)
    return tile(c64), tile(sp64), tile(sm64)


def _trunk(x, mem, in_g, in_b, w_in, wkv, lq1, lk1, lq2, lk2, subln_g, conv_w, wo, ln_g, ln_b, tables):
    q, k, v, sg, u, bg, qm, sgm = _in_proj(x, in_g, in_b, w_in, *tables)
    oda = _diff_attn(q, k, v, sg, lq1, lk1, lq2, lk2, subln_g)
    return _mix_out(x, oda, bg, qm, sgm, u, mem, wkv, conv_w, wo, in_g, in_b, ln_g, ln_b)


def kernel(x_prompt, x_sample, mem_prompt, mem_sample, in_ln_g, in_ln_b, w_in, w_mem_kv, lam_q1, lam_k1, lam_q2, lam_k2, subln_g, conv_w, w_o, ln_g, ln_b):
    assert w_in.shape[0] == 1, "single-layer trunk"
    assert x_prompt.shape[1] == x_sample.shape[1]
    tables = _rope_tables(x_prompt.shape[1])
    params = (in_ln_g[None, :], in_ln_b[None, :], w_in[0].astype(bf16), w_mem_kv[0].astype(bf16),
              lam_q1, lam_k1, lam_q2, lam_k2, subln_g, conv_w[0], w_o[0].astype(bf16), ln_g, ln_b)
    y_prompt = _trunk(x_prompt, mem_prompt, *params, tables)
    y_sample = _trunk(x_sample, mem_sample, *params, tables)
    return (y_prompt, y_sample)
```

```python
import math

import jax
import jax.numpy as jnp
from jax import lax
from jax.experimental import pallas as pl
from jax.experimental.pallas import tpu as pltpu

LANE = 128
DA_HEADS = 4
DA_DH = 64
DA_DV = 128
MX_HEADS = 4
MX_DH = 64
ROT_DIM = 16
ROPE_THETA = 500000.0
LN_EPS = 1e-5
LAM_INIT = 0.8 - 0.6 * math.exp(-0.3 * 0)
ALPHA = 2.0 ** 0.25
LOG2E = math.log2(math.e)

PROJ_ROWS = 512
ATTN_ROWS = 512
FINISH_ROWS = 32
KEY_TILE = 256
OUT_ROWS = 512
CONV_ROWS = 64
CONV_PAD = 8
VMEM_LIMIT = 56 * 1024 * 1024

f32 = jnp.float32
bf16 = jnp.bfloat16


def _layer_norm(x, g, b):
    mu = jnp.mean(x, axis=-1, keepdims=True)
    xc = x - mu
    var = jnp.mean(xc * xc, axis=-1, keepdims=True)
    return xc * lax.rsqrt(var + LN_EPS) * g + b


def _silu(x):
    return x / (1.0 + jnp.exp(-x))


def _in_proj_kernel(x_ref, g_ref, b_ref, w_ref, cos_ref, sinp_ref, sinm_ref,
                    q_ref, k_ref, v_ref, sg_ref, u_ref, bg_ref, qm_ref, sgm_ref, res_ref):
    t = pl.program_id(1)
    rows = x_ref.shape[1]
    h32 = _layer_norm(x_ref[0], g_ref[...], b_ref[...])
    res_ref[0] = ALPHA * h32
    h = h32.astype(bf16)

    r0 = pl.multiple_of(t * rows, rows)
    cos = cos_ref[pl.ds(r0, rows), :]
    sinp = sinp_ref[pl.ds(r0, rows), :]
    sinm = sinm_ref[pl.ds(r0, rows), :]

    def rope(c):
        return c * cos + pltpu.roll(c, 8, 1) * sinp + pltpu.roll(c, LANE - 8, 1) * sinm

    qk = jnp.dot(h, w_ref[:, 0:1024], preferred_element_type=f32)
    for j in range(DA_HEADS):
        q_ref[0, j] = (rope(qk[:, j * LANE:(j + 1) * LANE]) * (DA_DH ** -0.5 * LOG2E)).astype(bf16)
        k_ref[0, j] = rope(qk[:, 512 + j * LANE:512 + (j + 1) * LANE]).astype(bf16)

    vg = jnp.dot(h, w_ref[:, 1024:2048], preferred_element_type=f32)
    for j in range(DA_HEADS):
        v_ref[0, j] = vg[:, j * LANE:(j + 1) * LANE].astype(bf16)
        sg_ref[0, j] = _silu(vg[:, 512 + j * LANE:512 + (j + 1) * LANE]).astype(bf16)

    cv = jnp.dot(h, w_ref[:, 2048:3072], preferred_element_type=f32)
    u_ref[0] = cv[:, 256:512] * cv[:, 512:768]
    bg_ref[0] = (cv[:, 0:256] * _silu(cv[:, 768:1024])).astype(bf16)

    mx = jnp.dot(h, w_ref[:, 3072:3584], preferred_element_type=f32)
    qm_ref[0] = (mx[:, 0:256] * (MX_DH ** -0.5 * LOG2E)).astype(bf16)
    sgm_ref[0] = _silu(mx[:, 256:512]).astype(bf16)


def _in_proj(x, in_g, in_b, w_in, cos, sinp, sinm):
    B, S, D = x.shape
    rows = PROJ_ROWS
    grid = (B, S // rows)
    const = lambda b, t: (0, 0)
    head_spec = pl.BlockSpec((1, DA_HEADS, rows, LANE), lambda b, t: (b, 0, t, 0))
    row_spec = lambda w: pl.BlockSpec((1, rows, w), lambda b, t: (b, t, 0))
    head_shape = jax.ShapeDtypeStruct((B, DA_HEADS, S, LANE), bf16)
    return pl.pallas_call(
        _in_proj_kernel,
        grid=grid,
        in_specs=[
            pl.BlockSpec((1, rows, D), lambda b, t: (b, t, 0)),
            pl.BlockSpec((1, D), const),
            pl.BlockSpec((1, D), const),
            pl.BlockSpec(w_in.shape, const),
            pl.BlockSpec((S, LANE), const),
            pl.BlockSpec((S, LANE), const),
            pl.BlockSpec((S, LANE), const),
        ],
        out_specs=[head_spec, head_spec, head_spec, head_spec, row_spec(256),
                   row_spec(256), row_spec(256), row_spec(256), row_spec(D)],
        out_shape=[head_shape, head_shape, head_shape, head_shape,
                   jax.ShapeDtypeStruct((B, S, 256), f32),
                   jax.ShapeDtypeStruct((B, S, 256), bf16),
                   jax.ShapeDtypeStruct((B, S, 256), bf16),
                   jax.ShapeDtypeStruct((B, S, 256), bf16),
                   jax.ShapeDtypeStruct((B, S, D), f32)],
        compiler_params=pltpu.CompilerParams(
            dimension_semantics=("parallel", "arbitrary"),
            vmem_limit_bytes=VMEM_LIMIT),
        name="in_proj",
    )(x, in_g, in_b, w_in, cos, sinp, sinm)


def _diff_attn_kernel(q_ref, k_ref, v_ref, sg_ref, lq1_ref, lk1_ref, lq2_ref, lk2_ref, sub_ref,
                      o_ref, vext_ref, s0_ref, s1_ref, m0_ref, m1_ref):
    H, S = k_ref.shape[1], k_ref.shape[2]
    tq = ATTN_ROWS
    tiles_per_head = S // tq
    n_tiles = H * tiles_per_head
    for h in range(H):
        vext_ref[h, :, 0:DA_DV] = v_ref[0, h]
        vext_ref[h, :, DA_DV:2 * DA_DV] = jnp.ones((S, DA_DV), bf16)

    lam = (jnp.exp(jnp.sum(lq1_ref[...] * lk1_ref[...], axis=-1, keepdims=True))
           - jnp.exp(jnp.sum(lq2_ref[...] * lk2_ref[...], axis=-1, keepdims=True)) + LAM_INIT)
    gain = sub_ref[...] * (1.0 - LAM_INIT)
    first_map = lax.broadcasted_iota(jnp.int32, (tq, LANE), 1) < DA_DH

    def tile_pos(n):
        return n // tiles_per_head, pl.multiple_of((n % tiles_per_head) * tq, tq)

    key_tiles = S // KEY_TILE

    def scores_and_row_max(n, s_ref, m_ref):
        h, r0 = tile_pos(n)
        qh = q_ref[0, h, pl.ds(r0, tq), :]
        zero = jnp.zeros_like(qh)
        lhs = jnp.concatenate([jnp.where(first_map, qh, zero), jnp.where(first_map, zero, qh)], axis=0)
        m = None
        for j in range(key_tiles):
            keys = pl.ds(j * KEY_TILE, KEY_TILE)
            s = lax.dot_general(lhs, k_ref[0, h, keys, :], (((1,), (1,)), ((), ())), preferred_element_type=f32)
            s_ref[:, keys] = s
            mj = jnp.max(s, axis=-1, keepdims=True)
            m = mj if m is None else jnp.maximum(m, mj)
        m_ref[...] = jnp.broadcast_to(m, (2 * tq, LANE))

    def weighted_values(n, s_ref, m_ref):
        h, r0 = tile_pos(n)
        pv = None
        for j in range(key_tiles):
            halves = [jnp.exp2(s_ref[:, pl.ds(j * KEY_TILE + i * LANE, LANE)] - m_ref[...]).astype(bf16)
                      for i in range(KEY_TILE // LANE)]
            part = jnp.dot(jnp.concatenate(halves, axis=1), vext_ref[h, pl.ds(j * KEY_TILE, KEY_TILE), :],
                           preferred_element_type=f32)
            pv = part if pv is None else pv + part
        for c in range(tq // FINISH_ROWS):
            a, b = c * FINISH_ROWS, (c + 1) * FINISH_ROWS
            o = pv[a:b, :DA_DV] / pv[a:b, DA_DV:] - lam * (pv[tq + a:tq + b, :DA_DV] / pv[tq + a:tq + b, DA_DV:])
            o = o * lax.rsqrt(jnp.mean(o * o, axis=-1, keepdims=True) + LN_EPS) * gain
            rows = pl.ds(r0 + a, FINISH_ROWS)
            o_ref[0, h, rows, :] = (o * sg_ref[0, h, rows, :].astype(f32)).astype(bf16)

    scores_and_row_max(0, s0_ref, m0_ref)

    def steady(n, carry):
        @pl.when(n % 2 == 1)
        def _():
            scores_and_row_max(n, s1_ref, m1_ref)
            weighted_values(n - 1, s0_ref, m0_ref)

        @pl.when(n % 2 == 0)
        def _():
            scores_and_row_max(n, s0_ref, m0_ref)
            weighted_values(n - 1, s1_ref, m1_ref)

        return carry

    lax.fori_loop(1, n_tiles, steady, 0)
    weighted_values(n_tiles - 1, s1_ref, m1_ref)


def _diff_attn(q, k, v, sg, lq1, lk1, lq2, lk2, subln_g):
    B, H, S, _ = q.shape
    head_spec = pl.BlockSpec((1, H, S, LANE), lambda b: (b, 0, 0, 0))
    const = lambda b: (0, 0)
    return pl.pallas_call(
        _diff_attn_kernel,
        grid=(B,),
        in_specs=[head_spec, head_spec, head_spec, head_spec,
                  pl.BlockSpec((1, DA_DH), const), pl.BlockSpec((1, DA_DH), const),
                  pl.BlockSpec((1, DA_DH), const), pl.BlockSpec((1, DA_DH), const),
                  pl.BlockSpec((1, DA_DV), const)],
        out_specs=head_spec,
        out_shape=jax.ShapeDtypeStruct((B, H, S, DA_DV), bf16),
        scratch_shapes=[pltpu.VMEM((H, S, 2 * DA_DV), bf16),
                        pltpu.VMEM((2 * ATTN_ROWS, S), f32), pltpu.VMEM((2 * ATTN_ROWS, S), f32),
                        pltpu.VMEM((2 * ATTN_ROWS, LANE), f32), pltpu.VMEM((2 * ATTN_ROWS, LANE), f32)],
        compiler_params=pltpu.CompilerParams(
            dimension_semantics=("parallel",),
            vmem_limit_bytes=VMEM_LIMIT),
        name="diff_attn",
    )(q, k, v, sg, lq1, lk1, lq2, lk2, subln_g)


def _mix_out_kernel(res_ref, oda_ref, bg_ref, qm_ref, sgm_ref, u_ref, mem_ref, wkv_ref, cw_ref, wo_ref,
                    lng_ref, lnb_ref, y_ref, memk_ref, memv_ref, conv_ref, upad_ref):
    t = pl.program_id(1)
    rows = res_ref.shape[1]
    S = u_ref.shape[1]
    mxw = MX_HEADS * MX_DH

    @pl.when(t == 0)
    def _():
        kv = jnp.dot(mem_ref[0].astype(bf16), wkv_ref[...], preferred_element_type=f32)
        memk_ref[...] = kv[:, :mxw].astype(bf16)
        memv_ref[...] = kv[:, mxw:].astype(bf16)
        pad = jnp.zeros((CONV_PAD, u_ref.shape[2]), f32)
        upad_ref[0:CONV_PAD, :] = pad
        upad_ref[CONV_PAD + S:2 * CONV_PAD + S, :] = pad
        upad_ref[CONV_PAD:CONV_PAD + S, :] = u_ref[0]
        for c in range(S // CONV_ROWS):
            a = CONV_PAD + c * CONV_ROWS
            conv_ref[c * CONV_ROWS:(c + 1) * CONV_ROWS, :] = (
                upad_ref[a - 1:a - 1 + CONV_ROWS, :] * cw_ref[0:1, :]
                + upad_ref[a:a + CONV_ROWS, :] * cw_ref[1:2, :]
                + upad_ref[a + 1:a + 1 + CONV_ROWS, :] * cw_ref[2:3, :])

    r0 = pl.multiple_of(t * rows, rows)
    o_cv = (bg_ref[0].astype(f32) * conv_ref[pl.ds(r0, rows), :]).astype(bf16)

    qm = qm_ref[0]
    lane = lax.broadcasted_iota(jnp.int32, (rows, mxw), 1)
    zero = jnp.zeros_like(qm)
    head_of_lane = lane // MX_DH
    lhs = jnp.concatenate([jnp.where(head_of_lane == j, qm, zero) for j in range(MX_HEADS)], axis=0)
    s = lax.dot_general(lhs, memk_ref[...], (((1,), (1,)), ((), ())), preferred_element_type=f32)
    m = jnp.max(s, axis=-1, keepdims=True)
    p = jnp.exp2(s - m)
    l = jnp.sum(p, axis=-1, keepdims=True)
    r = jnp.dot(p.astype(bf16), memv_ref[...], preferred_element_type=f32) / l
    o_mx = jnp.zeros((rows, mxw), f32)
    for j in range(MX_HEADS):
        o_mx = jnp.where(head_of_lane == j, r[j * rows:(j + 1) * rows], o_mx)
    o_mx = (o_mx * sgm_ref[0].astype(f32)).astype(bf16)

    oda = jnp.concatenate([oda_ref[0, j] for j in range(DA_HEADS)], axis=-1)
    out = (jnp.dot(oda, wo_ref[0:512, :], preferred_element_type=f32)
           + jnp.dot(o_cv, wo_ref[512:768, :], preferred_element_type=f32)
           + jnp.dot(o_mx, wo_ref[768:1024, :], preferred_element_type=f32))
    y_ref[0] = _layer_norm(res_ref[0] + out, lng_ref[...], lnb_ref[...])


def _mix_out(res, oda, bg, qm, sgm, u, mem, wkv, conv_w, wo, ln_g, ln_b):
    B, S, D = res.shape
    rows = OUT_ROWS
    n_mem = mem.shape[1]
    const = lambda b, t: (0, 0)
    row_spec = lambda w: pl.BlockSpec((1, rows, w), lambda b, t: (b, t, 0))
    return pl.pallas_call(
        _mix_out_kernel,
        grid=(B, S // rows),
        in_specs=[row_spec(D), pl.BlockSpec((1, DA_HEADS, rows, LANE), lambda b, t: (b, 0, t, 0)),
                  row_spec(256), row_spec(256), row_spec(256),
                  pl.BlockSpec((1, S, 256), lambda b, t: (b, 0, 0)),
                  pl.BlockSpec((1, n_mem, D), lambda b, t: (b, 0, 0)),
                  pl.BlockSpec(wkv.shape, const),
                  pl.BlockSpec(conv_w.shape, const),
                  pl.BlockSpec(wo.shape, const),
                  pl.BlockSpec((1, D), const), pl.BlockSpec((1, D), const)],
        out_specs=row_spec(D),
        out_shape=jax.ShapeDtypeStruct((B, S, D), f32),
        scratch_shapes=[pltpu.VMEM((n_mem, MX_HEADS * MX_DH), bf16),
                        pltpu.VMEM((n_mem, MX_HEADS * MX_DH), bf16),
                        pltpu.VMEM((S, 256), f32),
                        pltpu.VMEM((S + 2 * CONV_PAD, 256), f32)],
        compiler_params=pltpu.CompilerParams(
            dimension_semantics=("parallel", "arbitrary"),
            vmem_limit_bytes=VMEM_LIMIT),
        name="mix_out",
    )(res, oda, bg, qm, sgm, u, mem, wkv, conv_w, wo, ln_g, ln_b)


def _rope_tables(seq_len):
    inv_freq = ROPE_THETA ** (-jnp.arange(0, ROT_DIM, 2, dtype=f32) / ROT_DIM)
    ang = jnp.arange(seq_len, dtype=f32)[:, None] * inv_freq[None, :]
    cos, sin = jnp.cos(ang), jnp.sin(ang)
    half = ROT_DIM // 2
    pad = DA_DH - ROT_DIM
    ones = jnp.ones((seq_len, pad), f32)
    zeros_h = jnp.zeros((seq_len, half), f32)
    zeros_p = jnp.zeros((seq_len, pad), f32)
    c64 = jnp.concatenate([cos, cos, ones], axis=-1)
    sp64 = jnp.concatenate([zeros_h, sin, zeros_p], axis=-1)
    sm64 = jnp.concatenate([-sin, zeros_h, zeros_p], axis=-1)
    tile = lambda a: jnp.concatenate([a, a], axis=-1)
    return tile(c64), tile(sp64), tile(sm64)


def _trunk(x, mem, in_g, in_b, w_in, wkv, lq1, lk1, lq2, lk2, subln_g, conv_w, wo, ln_g, ln_b, tables):
    q, k, v, sg, u, bg, qm, sgm, res = _in_proj(x, in_g, in_b, w_in, *tables)
    oda = _diff_attn(q, k, v, sg, lq1, lk1, lq2, lk2, subln_g)
    return _mix_out(res, oda, bg, qm, sgm, u, mem, wkv, conv_w, wo, ln_g, ln_b)


def kernel(x_prompt, x_sample, mem_prompt, mem_sample, in_ln_g, in_ln_b, w_in, w_mem_kv, lam_q1, lam_k1, lam_q2, lam_k2, subln_g, conv_w, w_o, ln_g, ln_b):
    assert w_in.shape[0] == 1, "single-layer trunk"
    assert x_prompt.shape[1] == x_sample.shape[1]
    tables = _rope_tables(x_prompt.shape[1])
    params = (in_ln_g[None, :], in_ln_b[None, :], w_in[0].astype(bf16), w_mem_kv[0].astype(bf16),
              lam_q1, lam_k1, lam_q2, lam_k2, subln_g, conv_w[0], w_o[0].astype(bf16), ln_g, ln_b)
    y_prompt = _trunk(x_prompt, mem_prompt, *params, tables)
    y_sample = _trunk(x_sample, mem_sample, *params, tables)
    return (y_prompt, y_sample)
```

```python
import math

import jax
import jax.numpy as jnp
from jax import lax
from jax.experimental import pallas as pl
from jax.experimental.pallas import tpu as pltpu

LANE = 128
DA_HEADS = 4
DA_DH = 64
DA_DV = 128
MX_HEADS = 4
MX_DH = 64
ROT_DIM = 16
ROPE_THETA = 500000.0
LN_EPS = 1e-5
LAM_INIT = 0.8 - 0.6 * math.exp(-0.3 * 0)
ALPHA = 2.0 ** 0.25
LOG2E = math.log2(math.e)

PROJ_ROWS = 1024
PROJ_SUB = 512
ATTN_ROWS = 512
FINISH_ROWS = 32
KEY_TILE = 256
OUT_ROWS = 1024
OUT_SUB = 512
CONV_ROWS = 64
CONV_PAD = 8
VMEM_LIMIT = 56 * 1024 * 1024

f32 = jnp.float32
bf16 = jnp.bfloat16


def _layer_norm(x, g, b):
    mu = jnp.mean(x, axis=-1, keepdims=True)
    xc = x - mu
    var = jnp.mean(xc * xc, axis=-1, keepdims=True)
    return xc * lax.rsqrt(var + LN_EPS) * g + b


def _silu(x):
    return x / (1.0 + jnp.exp(-x))


def _in_proj_kernel(x_ref, g_ref, b_ref, w_ref, cos_ref, sinp_ref, sinm_ref,
                    q_ref, k_ref, v_ref, sg_ref, u_ref, bg_ref, qm_ref, sgm_ref, res_ref):
    t = pl.program_id(1)
    rows = x_ref.shape[1]

    def rope(c, cos, sinp, sinm):
        return c * cos + pltpu.roll(c, 8, 1) * sinp + pltpu.roll(c, LANE - 8, 1) * sinm

    for s in range(rows // PROJ_SUB):
        sub = pl.ds(s * PROJ_SUB, PROJ_SUB)
        h32 = _layer_norm(x_ref[0, sub, :], g_ref[...], b_ref[...])
        res_ref[0, sub, :] = ALPHA * h32
        h = h32.astype(bf16)

        pos = pl.ds(pl.multiple_of(t * rows + s * PROJ_SUB, PROJ_SUB), PROJ_SUB)
        tabs = (cos_ref[pos, :], sinp_ref[pos, :], sinm_ref[pos, :])

        qk = jnp.dot(h, w_ref[:, 0:1024], preferred_element_type=f32)
        for j in range(DA_HEADS):
            q_ref[0, j, sub, :] = (rope(qk[:, j * LANE:(j + 1) * LANE], *tabs) * (DA_DH ** -0.5 * LOG2E)).astype(bf16)
            k_ref[0, j, sub, :] = rope(qk[:, 512 + j * LANE:512 + (j + 1) * LANE], *tabs).astype(bf16)

        vg = jnp.dot(h, w_ref[:, 1024:2048], preferred_element_type=f32)
        for j in range(DA_HEADS):
            v_ref[0, j, sub, :] = vg[:, j * LANE:(j + 1) * LANE].astype(bf16)
            sg_ref[0, j, sub, :] = _silu(vg[:, 512 + j * LANE:512 + (j + 1) * LANE]).astype(bf16)

        cv = jnp.dot(h, w_ref[:, 2048:3072], preferred_element_type=f32)
        u_ref[0, sub, :] = cv[:, 256:512] * cv[:, 512:768]
        bg_ref[0, sub, :] = (cv[:, 0:256] * _silu(cv[:, 768:1024])).astype(bf16)

        mx = jnp.dot(h, w_ref[:, 3072:3584], preferred_element_type=f32)
        qm_ref[0, sub, :] = (mx[:, 0:256] * (MX_DH ** -0.5 * LOG2E)).astype(bf16)
        sgm_ref[0, sub, :] = _silu(mx[:, 256:512]).astype(bf16)


def _in_proj(x, in_g, in_b, w_in, cos, sinp, sinm):
    B, S, D = x.shape
    rows = PROJ_ROWS
    grid = (B, S // rows)
    const = lambda b, t: (0, 0)
    resident = lambda shape: pl.BlockSpec(shape, const, pipeline_mode=pl.Buffered(1))
    head_spec = pl.BlockSpec((1, DA_HEADS, rows, LANE), lambda b, t: (b, 0, t, 0))
    row_spec = lambda w: pl.BlockSpec((1, rows, w), lambda b, t: (b, t, 0))
    head_shape = jax.ShapeDtypeStruct((B, DA_HEADS, S, LANE), bf16)
    return pl.pallas_call(
        _in_proj_kernel,
        grid=grid,
        in_specs=[
            pl.BlockSpec((1, rows, D), lambda b, t: (b, t, 0)),
            resident((1, D)), resident((1, D)), resident(w_in.shape),
            resident((S, LANE)), resident((S, LANE)), resident((S, LANE)),
        ],
        out_specs=[head_spec, head_spec, head_spec, head_spec, row_spec(256),
                   row_spec(256), row_spec(256), row_spec(256), row_spec(D)],
        out_shape=[head_shape, head_shape, head_shape, head_shape,
                   jax.ShapeDtypeStruct((B, S, 256), f32),
                   jax.ShapeDtypeStruct((B, S, 256), bf16),
                   jax.ShapeDtypeStruct((B, S, 256), bf16),
                   jax.ShapeDtypeStruct((B, S, 256), bf16),
                   jax.ShapeDtypeStruct((B, S, D), f32)],
        compiler_params=pltpu.CompilerParams(
            dimension_semantics=("parallel", "arbitrary"),
            vmem_limit_bytes=VMEM_LIMIT),
        name="in_proj",
    )(x, in_g, in_b, w_in, cos, sinp, sinm)


def _diff_attn_kernel(q_ref, k_ref, v_ref, sg_ref, lq1_ref, lk1_ref, lq2_ref, lk2_ref, sub_ref,
                      o_ref, vext_ref, s0_ref, s1_ref, m0_ref, m1_ref):
    H, S = k_ref.shape[1], k_ref.shape[2]
    tq = ATTN_ROWS
    tiles_per_head = S // tq
    n_tiles = H * tiles_per_head
    for h in range(H):
        vext_ref[h, :, 0:DA_DV] = v_ref[0, h]
        vext_ref[h, :, DA_DV:2 * DA_DV] = jnp.ones((S, DA_DV), bf16)

    lam = (jnp.exp(jnp.sum(lq1_ref[...] * lk1_ref[...], axis=-1, keepdims=True))
           - jnp.exp(jnp.sum(lq2_ref[...] * lk2_ref[...], axis=-1, keepdims=True)) + LAM_INIT)
    gain = sub_ref[...] * (1.0 - LAM_INIT)
    first_map = lax.broadcasted_iota(jnp.int32, (tq, LANE), 1) < DA_DH

    def tile_pos(n):
        return n // tiles_per_head, pl.multiple_of((n % tiles_per_head) * tq, tq)

    key_tiles = S // KEY_TILE

    def scores_and_row_max(n, s_ref, m_ref):
        h, r0 = tile_pos(n)
        qh = q_ref[0, h, pl.ds(r0, tq), :]
        zero = jnp.zeros_like(qh)
        lhs = jnp.concatenate([jnp.where(first_map, qh, zero), jnp.where(first_map, zero, qh)], axis=0)
        m = None
        for j in range(key_tiles):
            keys = pl.ds(j * KEY_TILE, KEY_TILE)
            s = lax.dot_general(lhs, k_ref[0, h, keys, :], (((1,), (1,)), ((), ())), preferred_element_type=f32)
            s_ref[:, keys] = s
            mj = jnp.max(s, axis=-1, keepdims=True)
            m = mj if m is None else jnp.maximum(m, mj)
        m_ref[...] = jnp.broadcast_to(m, (2 * tq, LANE))

    def weighted_values(n, s_ref, m_ref):
        h, r0 = tile_pos(n)
        pv = None
        for j in range(key_tiles):
            halves = [jnp.exp2(s_ref[:, pl.ds(j * KEY_TILE + i * LANE, LANE)] - m_ref[...]).astype(bf16)
                      for i in range(KEY_TILE // LANE)]
            part = jnp.dot(jnp.concatenate(halves, axis=1), vext_ref[h, pl.ds(j * KEY_TILE, KEY_TILE), :],
                           preferred_element_type=f32)
            pv = part if pv is None else pv + part
        for c in range(tq // FINISH_ROWS):
            a, b = c * FINISH_ROWS, (c + 1) * FINISH_ROWS
            o = pv[a:b, :DA_DV] / pv[a:b, DA_DV:] - lam * (pv[tq + a:tq + b, :DA_DV] / pv[tq + a:tq + b, DA_DV:])
            o = o * lax.rsqrt(jnp.mean(o * o, axis=-1, keepdims=True) + LN_EPS) * gain
            rows = pl.ds(r0 + a, FINISH_ROWS)
            o_ref[0, h, rows, :] = (o * sg_ref[0, h, rows, :].astype(f32)).astype(bf16)

    scores_and_row_max(0, s0_ref, m0_ref)

    def steady(n, carry):
        @pl.when(n % 2 == 1)
        def _():
            scores_and_row_max(n, s1_ref, m1_ref)
            weighted_values(n - 1, s0_ref, m0_ref)

        @pl.when(n % 2 == 0)
        def _():
            scores_and_row_max(n, s0_ref, m0_ref)
            weighted_values(n - 1, s1_ref, m1_ref)

        return carry

    lax.fori_loop(1, n_tiles, steady, 0)
    weighted_values(n_tiles - 1, s1_ref, m1_ref)


def _diff_attn(q, k, v, sg, lq1, lk1, lq2, lk2, subln_g):
    B, H, S, _ = q.shape
    head_spec = pl.BlockSpec((1, H, S, LANE), lambda b: (b, 0, 0, 0))
    const = lambda b: (0, 0)
    return pl.pallas_call(
        _diff_attn_kernel,
        grid=(B,),
        in_specs=[head_spec, head_spec, head_spec, head_spec,
                  pl.BlockSpec((1, DA_DH), const), pl.BlockSpec((1, DA_DH), const),
                  pl.BlockSpec((1, DA_DH), const), pl.BlockSpec((1, DA_DH), const),
                  pl.BlockSpec((1, DA_DV), const)],
        out_specs=head_spec,
        out_shape=jax.ShapeDtypeStruct((B, H, S, DA_DV), bf16),
        scratch_shapes=[pltpu.VMEM((H, S, 2 * DA_DV), bf16),
                        pltpu.VMEM((2 * ATTN_ROWS, S), f32), pltpu.VMEM((2 * ATTN_ROWS, S), f32),
                        pltpu.VMEM((2 * ATTN_ROWS, LANE), f32), pltpu.VMEM((2 * ATTN_ROWS, LANE), f32)],
        compiler_params=pltpu.CompilerParams(
            dimension_semantics=("parallel",),
            vmem_limit_bytes=VMEM_LIMIT),
        name="diff_attn",
    )(q, k, v, sg, lq1, lk1, lq2, lk2, subln_g)


def _mix_out_kernel(res_ref, oda_ref, bg_ref, qm_ref, sgm_ref, u_ref, mem_ref, wkv_ref, cw_ref, wo_ref,
                    lng_ref, lnb_ref, y_ref, memk_ref, memv_ref, conv_ref, upad_ref):
    t = pl.program_id(1)
    rows = res_ref.shape[1]
    S = u_ref.shape[1]
    mxw = MX_HEADS * MX_DH

    @pl.when(t == 0)
    def _():
        kv = jnp.dot(mem_ref[0].astype(bf16), wkv_ref[...], preferred_element_type=f32)
        memk_ref[...] = kv[:, :mxw].astype(bf16)
        memv_ref[...] = kv[:, mxw:].astype(bf16)
        pad = jnp.zeros((CONV_PAD, u_ref.shape[2]), f32)
        upad_ref[0:CONV_PAD, :] = pad
        upad_ref[CONV_PAD + S:2 * CONV_PAD + S, :] = pad
        upad_ref[CONV_PAD:CONV_PAD + S, :] = u_ref[0]
        for c in range(S // CONV_ROWS):
            a = CONV_PAD + c * CONV_ROWS
            conv_ref[c * CONV_ROWS:(c + 1) * CONV_ROWS, :] = (
                upad_ref[a - 1:a - 1 + CONV_ROWS, :] * cw_ref[0:1, :]
                + upad_ref[a:a + CONV_ROWS, :] * cw_ref[1:2, :]
                + upad_ref[a + 1:a + 1 + CONV_ROWS, :] * cw_ref[2:3, :])

    lane = lax.broadcasted_iota(jnp.int32, (OUT_SUB, mxw), 1)
    head_of_lane = lane // MX_DH

    for s in range(rows // OUT_SUB):
        sub = pl.ds(s * OUT_SUB, OUT_SUB)
        pos = pl.ds(pl.multiple_of(t * rows + s * OUT_SUB, OUT_SUB), OUT_SUB)
        o_cv = (bg_ref[0, sub, :].astype(f32) * conv_ref[pos, :]).astype(bf16)

        qm = qm_ref[0, sub, :]
        zero = jnp.zeros_like(qm)
        lhs = jnp.concatenate([jnp.where(head_of_lane == j, qm, zero) for j in range(MX_HEADS)], axis=0)
        sc = lax.dot_general(lhs, memk_ref[...], (((1,), (1,)), ((), ())), preferred_element_type=f32)
        m = jnp.max(sc, axis=-1, keepdims=True)
        p = jnp.exp2(sc - m)
        l = jnp.sum(p, axis=-1, keepdims=True)
        r = jnp.dot(p.astype(bf16), memv_ref[...], preferred_element_type=f32) / l
        o_mx = jnp.zeros((OUT_SUB, mxw), f32)
        for j in range(MX_HEADS):
            o_mx = jnp.where(head_of_lane == j, r[j * OUT_SUB:(j + 1) * OUT_SUB], o_mx)
        o_mx = (o_mx * sgm_ref[0, sub, :].astype(f32)).astype(bf16)

        oda = jnp.concatenate([oda_ref[0, j, sub, :] for j in range(DA_HEADS)], axis=-1)
        out = (jnp.dot(oda, wo_ref[0:512, :], preferred_element_type=f32)
               + jnp.dot(o_cv, wo_ref[512:768, :], preferred_element_type=f32)
               + jnp.dot(o_mx, wo_ref[768:1024, :], preferred_element_type=f32))
        y_ref[0, sub, :] = _layer_norm(res_ref[0, sub, :] + out, lng_ref[...], lnb_ref[...])


def _mix_out(res, oda, bg, qm, sgm, u, mem, wkv, conv_w, wo, ln_g, ln_b):
    B, S, D = res.shape
    rows = OUT_ROWS
    n_mem = mem.shape[1]
    const = lambda b, t: (0, 0)
    row_spec = lambda w: pl.BlockSpec((1, rows, w), lambda b, t: (b, t, 0))
    return pl.pallas_call(
        _mix_out_kernel,
        grid=(B, S // rows),
        in_specs=[row_spec(D), pl.BlockSpec((1, DA_HEADS, rows, LANE), lambda b, t: (b, 0, t, 0)),
                  row_spec(256), row_spec(256), row_spec(256),
                  pl.BlockSpec((1, S, 256), lambda b, t: (b, 0, 0)),
                  pl.BlockSpec((1, n_mem, D), lambda b, t: (b, 0, 0)),
                  pl.BlockSpec(wkv.shape, const),
                  pl.BlockSpec(conv_w.shape, const),
                  pl.BlockSpec(wo.shape, const),
                  pl.BlockSpec((1, D), const), pl.BlockSpec((1, D), const)],
        out_specs=row_spec(D),
        out_shape=jax.ShapeDtypeStruct((B, S, D), f32),
        scratch_shapes=[pltpu.VMEM((n_mem, MX_HEADS * MX_DH), bf16),
                        pltpu.VMEM((n_mem, MX_HEADS * MX_DH), bf16),
                        pltpu.VMEM((S, 256), f32),
                        pltpu.VMEM((S + 2 * CONV_PAD, 256), f32)],
        compiler_params=pltpu.CompilerParams(
            dimension_semantics=("parallel", "arbitrary"),
            vmem_limit_bytes=VMEM_LIMIT),
        name="mix_out",
    )(res, oda, bg, qm, sgm, u, mem, wkv, conv_w, wo, ln_g, ln_b)


def _rope_tables(seq_len):
    inv_freq = ROPE_THETA ** (-jnp.arange(0, ROT_DIM, 2, dtype=f32) / ROT_DIM)
    ang = jnp.arange(seq_len, dtype=f32)[:, None] * inv_freq[None, :]
    cos, sin = jnp.cos(ang), jnp.sin(ang)
    half = ROT_DIM // 2
    pad = DA_DH - ROT_DIM
    ones = jnp.ones((seq_len, pad), f32)
    zeros_h = jnp.zeros((seq_len, half), f32)
    zeros_p = jnp.zeros((seq_len, pad), f32)
    c64 = jnp.concatenate([cos, cos, ones], axis=-1)
    sp64 = jnp.concatenate([zeros_h, sin, zeros_p], axis=-1)
    sm64 = jnp.concatenate([-sin, zeros_h, zeros_p], axis=-1)
    tile = lambda a: jnp.concatenate([a, a], axis=-1)
    return tile(c64), tile(sp64), tile(sm64)


def _trunk(x, mem, in_g, in_b, w_in, wkv, lq1, lk1, lq2, lk2, subln_g, conv_w, wo, ln_g, ln_b, tables):
    q, k, v, sg, u, bg, qm, sgm, res = _in_proj(x, in_g, in_b, w_in, *tables)
    oda = _diff_attn(q, k, v, sg, lq1, lk1, lq2, lk2, subln_g)
    return _mix_out(res, oda, bg, qm, sgm, u, mem, wkv, conv_w, wo, ln_g, ln_b)


def kernel(x_prompt, x_sample, mem_prompt, mem_sample, in_ln_g, in_ln_b, w_in, w_mem_kv, lam_q1, lam_k1, lam_q2, lam_k2, subln_g, conv_w, w_o, ln_g, ln_b):
    assert w_in.shape[0] == 1, "single-layer trunk"
    assert x_prompt.shape[1] == x_sample.shape[1]
    tables = _rope_tables(x_prompt.shape[1])
    params = (in_ln_g[None, :], in_ln_b[None, :], w_in[0].astype(bf16), w_mem_kv[0].astype(bf16),
              lam_q1, lam_k1, lam_q2, lam_k2, subln_g, conv_w[0], w_o[0].astype(bf16), ln_g, ln_b)
    y_prompt = _trunk(x_prompt, mem_prompt, *params, tables)
    y_sample = _trunk(x_sample, mem_sample, *params, tables)
    return (y_prompt, y_sample)
```

```python
import math

import jax
import jax.numpy as jnp
from jax import lax
from jax.experimental import pallas as pl
from jax.experimental.pallas import tpu as pltpu

LANE = 128
DA_HEADS = 4
DA_DH = 64
DA_DV = 128
MX_HEADS = 4
MX_DH = 64
ROT_DIM = 16
ROPE_THETA = 500000.0
LN_EPS = 1e-5
LAM_INIT = 0.8 - 0.6 * math.exp(-0.3 * 0)
ALPHA = 2.0 ** 0.25
LOG2E = math.log2(math.e)

PROJ_ROWS = 1024
PROJ_SUB = 512
ATTN_ROWS = 512
SCORE_BUFFERS = 3
FINISH_ROWS = 32
KEY_TILE = 256
OUT_ROWS = 1024
OUT_SUB = 512
CONV_ROWS = 64
CONV_PAD = 8
VMEM_LIMIT = 56 * 1024 * 1024

f32 = jnp.float32
bf16 = jnp.bfloat16


def _layer_norm(x, g, b):
    mu = jnp.mean(x, axis=-1, keepdims=True)
    xc = x - mu
    var = jnp.mean(xc * xc, axis=-1, keepdims=True)
    return xc * lax.rsqrt(var + LN_EPS) * g + b


def _silu(x):
    return x / (1.0 + jnp.exp(-x))


def _in_proj_kernel(x_ref, g_ref, b_ref, w_ref, cos_ref, sinp_ref, sinm_ref,
                    q_ref, k_ref, v_ref, sg_ref, u_ref, bg_ref, qm_ref, sgm_ref, res_ref):
    t = pl.program_id(1)
    rows = x_ref.shape[1]

    def rope(c, cos, sinp, sinm):
        return c * cos + pltpu.roll(c, 8, 1) * sinp + pltpu.roll(c, LANE - 8, 1) * sinm

    for s in range(rows // PROJ_SUB):
        sub = pl.ds(s * PROJ_SUB, PROJ_SUB)
        h32 = _layer_norm(x_ref[0, sub, :], g_ref[...], b_ref[...])
        res_ref[0, sub, :] = ALPHA * h32
        h = h32.astype(bf16)

        pos = pl.ds(pl.multiple_of(t * rows + s * PROJ_SUB, PROJ_SUB), PROJ_SUB)
        tabs = (cos_ref[pos, :], sinp_ref[pos, :], sinm_ref[pos, :])

        qk = jnp.dot(h, w_ref[:, 0:1024], preferred_element_type=f32)
        for j in range(DA_HEADS):
            q_ref[0, j, sub, :] = (rope(qk[:, j * LANE:(j + 1) * LANE], *tabs) * (DA_DH ** -0.5 * LOG2E)).astype(bf16)
            k_ref[0, j, sub, :] = rope(qk[:, 512 + j * LANE:512 + (j + 1) * LANE], *tabs).astype(bf16)

        vg = jnp.dot(h, w_ref[:, 1024:2048], preferred_element_type=f32)
        for j in range(DA_HEADS):
            v_ref[0, j, sub, :] = vg[:, j * LANE:(j + 1) * LANE].astype(bf16)
            sg_ref[0, j, sub, :] = _silu(vg[:, 512 + j * LANE:512 + (j + 1) * LANE]).astype(bf16)

        cv = jnp.dot(h, w_ref[:, 2048:3072], preferred_element_type=f32)
        u_ref[0, sub, :] = cv[:, 256:512] * cv[:, 512:768]
        bg_ref[0, sub, :] = (cv[:, 0:256] * _silu(cv[:, 768:1024])).astype(bf16)

        mx = jnp.dot(h, w_ref[:, 3072:3584], preferred_element_type=f32)
        qm_ref[0, sub, :] = (mx[:, 0:256] * (MX_DH ** -0.5 * LOG2E)).astype(bf16)
        sgm_ref[0, sub, :] = _silu(mx[:, 256:512]).astype(bf16)


def _in_proj(x, in_g, in_b, w_in, cos, sinp, sinm):
    B, S, D = x.shape
    rows = PROJ_ROWS
    grid = (B, S // rows)
    const = lambda b, t: (0, 0)
    resident = lambda shape: pl.BlockSpec(shape, const, pipeline_mode=pl.Buffered(1))
    head_spec = pl.BlockSpec((1, DA_HEADS, rows, LANE), lambda b, t: (b, 0, t, 0))
    row_spec = lambda w: pl.BlockSpec((1, rows, w), lambda b, t: (b, t, 0))
    head_shape = jax.ShapeDtypeStruct((B, DA_HEADS, S, LANE), bf16)
    return pl.pallas_call(
        _in_proj_kernel,
        grid=grid,
        in_specs=[
            pl.BlockSpec((1, rows, D), lambda b, t: (b, t, 0)),
            resident((1, D)), resident((1, D)), resident(w_in.shape),
            resident((S, LANE)), resident((S, LANE)), resident((S, LANE)),
        ],
        out_specs=[head_spec, head_spec, head_spec, head_spec, row_spec(256),
                   row_spec(256), row_spec(256), row_spec(256), row_spec(D)],
        out_shape=[head_shape, head_shape, head_shape, head_shape,
                   jax.ShapeDtypeStruct((B, S, 256), f32),
                   jax.ShapeDtypeStruct((B, S, 256), bf16),
                   jax.ShapeDtypeStruct((B, S, 256), bf16),
                   jax.ShapeDtypeStruct((B, S, 256), bf16),
                   jax.ShapeDtypeStruct((B, S, D), f32)],
        compiler_params=pltpu.CompilerParams(
            dimension_semantics=("parallel", "arbitrary"),
            vmem_limit_bytes=VMEM_LIMIT),
        name="in_proj",
    )(x, in_g, in_b, w_in, cos, sinp, sinm)


def _diff_attn_kernel(q_ref, k_ref, v_ref, sg_ref, lq1_ref, lk1_ref, lq2_ref, lk2_ref, sub_ref,
                      o_ref, vext_ref, *score_refs):
    bufs = list(zip(score_refs[:SCORE_BUFFERS], score_refs[SCORE_BUFFERS:]))
    H, S = k_ref.shape[1], k_ref.shape[2]
    tq = ATTN_ROWS
    tiles_per_head = S // tq
    n_tiles = H * tiles_per_head
    for h in range(H):
        vext_ref[h, :, 0:DA_DV] = v_ref[0, h]
        vext_ref[h, :, DA_DV:2 * DA_DV] = jnp.ones((S, DA_DV), bf16)

    lam = (jnp.exp(jnp.sum(lq1_ref[...] * lk1_ref[...], axis=-1, keepdims=True))
           - jnp.exp(jnp.sum(lq2_ref[...] * lk2_ref[...], axis=-1, keepdims=True)) + LAM_INIT)
    gain = sub_ref[...] * (1.0 - LAM_INIT)
    first_map = lax.broadcasted_iota(jnp.int32, (tq, LANE), 1) < DA_DH

    def tile_pos(n):
        return n // tiles_per_head, pl.multiple_of((n % tiles_per_head) * tq, tq)

    key_tiles = S // KEY_TILE

    def scores_and_row_max(n, s_ref, m_ref):
        h, r0 = tile_pos(n)
        qh = q_ref[0, h, pl.ds(r0, tq), :]
        zero = jnp.zeros_like(qh)
        lhs = jnp.concatenate([jnp.where(first_map, qh, zero), jnp.where(first_map, zero, qh)], axis=0)
        m = None
        for j in range(key_tiles):
            keys = pl.ds(j * KEY_TILE, KEY_TILE)
            s = lax.dot_general(lhs, k_ref[0, h, keys, :], (((1,), (1,)), ((), ())), preferred_element_type=f32)
            s_ref[:, keys] = s
            for i in range(KEY_TILE // LANE):
                part = s[:, i * LANE:(i + 1) * LANE]
                m = part if m is None else jnp.maximum(m, part)
        m = jnp.max(m, axis=-1, keepdims=True)
        m_ref[...] = jnp.broadcast_to(m, (2 * tq, LANE))

    def weighted_values(n, s_ref, m_ref):
        h, r0 = tile_pos(n)
        pv = None
        for j in range(key_tiles):
            halves = [jnp.exp2(s_ref[:, pl.ds(j * KEY_TILE + i * LANE, LANE)] - m_ref[...]).astype(bf16)
                      for i in range(KEY_TILE // LANE)]
            part = jnp.dot(jnp.concatenate(halves, axis=1), vext_ref[h, pl.ds(j * KEY_TILE, KEY_TILE), :],
                           preferred_element_type=f32)
            pv = part if pv is None else pv + part
        for c in range(tq // FINISH_ROWS):
            a, b = c * FINISH_ROWS, (c + 1) * FINISH_ROWS
            o = pv[a:b, :DA_DV] / pv[a:b, DA_DV:] - lam * (pv[tq + a:tq + b, :DA_DV] / pv[tq + a:tq + b, DA_DV:])
            o = o * lax.rsqrt(jnp.mean(o * o, axis=-1, keepdims=True) + LN_EPS) * gain
            rows = pl.ds(r0 + a, FINISH_ROWS)
            o_ref[0, h, rows, :] = (o * sg_ref[0, h, rows, :].astype(f32)).astype(bf16)

    scores_and_row_max(0, *bufs[0])

    def steady(i, carry):
        n = len(bufs) * i + 1
        for d in range(len(bufs)):
            scores_and_row_max(n + d, *bufs[(1 + d) % len(bufs)])
            weighted_values(n + d - 1, *bufs[d % len(bufs)])
        return carry

    assert (n_tiles - 1) % len(bufs) == 0
    lax.fori_loop(0, (n_tiles - 1) // len(bufs), steady, 0)
    weighted_values(n_tiles - 1, *bufs[(n_tiles - 1) % len(bufs)])


def _diff_attn(q, k, v, sg, lq1, lk1, lq2, lk2, subln_g):
    B, H, S, _ = q.shape
    head_spec = pl.BlockSpec((1, H, S, LANE), lambda b: (b, 0, 0, 0))
    const = lambda b: (0, 0)
    return pl.pallas_call(
        _diff_attn_kernel,
        grid=(B,),
        in_specs=[head_spec, head_spec, head_spec, head_spec,
                  pl.BlockSpec((1, DA_DH), const), pl.BlockSpec((1, DA_DH), const),
                  pl.BlockSpec((1, DA_DH), const), pl.BlockSpec((1, DA_DH), const),
                  pl.BlockSpec((1, DA_DV), const)],
        out_specs=head_spec,
        out_shape=jax.ShapeDtypeStruct((B, H, S, DA_DV), bf16),
        scratch_shapes=[pltpu.VMEM((H, S, 2 * DA_DV), bf16),
                        *[pltpu.VMEM((2 * ATTN_ROWS, S), f32) for _ in range(SCORE_BUFFERS)],
                        *[pltpu.VMEM((2 * ATTN_ROWS, LANE), f32) for _ in range(SCORE_BUFFERS)]],
        compiler_params=pltpu.CompilerParams(
            dimension_semantics=("parallel",),
            vmem_limit_bytes=VMEM_LIMIT),
        name="diff_attn",
    )(q, k, v, sg, lq1, lk1, lq2, lk2, subln_g)


def _mix_out_kernel(res_ref, oda_ref, bg_ref, qm_ref, sgm_ref, u_ref, mem_ref, wkv_ref, cw_ref, wo_ref,
                    lng_ref, lnb_ref, y_ref, memk_ref, memv_ref, conv_ref, upad_ref):
    t = pl.program_id(1)
    rows = res_ref.shape[1]
    S = u_ref.shape[1]
    mxw = MX_HEADS * MX_DH

    @pl.when(t == 0)
    def _():
        kv = jnp.dot(mem_ref[0].astype(bf16), wkv_ref[...], preferred_element_type=f32)
        memk_ref[...] = kv[:, :mxw].astype(bf16)
        memv_ref[...] = kv[:, mxw:].astype(bf16)
        pad = jnp.zeros((CONV_PAD, u_ref.shape[2]), f32)
        upad_ref[0:CONV_PAD, :] = pad
        upad_ref[CONV_PAD + S:2 * CONV_PAD + S, :] = pad
        upad_ref[CONV_PAD:CONV_PAD + S, :] = u_ref[0]
        for c in range(S // CONV_ROWS):
            a = CONV_PAD + c * CONV_ROWS
            conv_ref[c * CONV_ROWS:(c + 1) * CONV_ROWS, :] = (
                upad_ref[a - 1:a - 1 + CONV_ROWS, :] * cw_ref[0:1, :]
                + upad_ref[a:a + CONV_ROWS, :] * cw_ref[1:2, :]
                + upad_ref[a + 1:a + 1 + CONV_ROWS, :] * cw_ref[2:3, :])

    lane = lax.broadcasted_iota(jnp.int32, (OUT_SUB, mxw), 1)
    head_of_lane = lane // MX_DH

    for s in range(rows // OUT_SUB):
        sub = pl.ds(s * OUT_SUB, OUT_SUB)
        pos = pl.ds(pl.multiple_of(t * rows + s * OUT_SUB, OUT_SUB), OUT_SUB)
        o_cv = (bg_ref[0, sub, :].astype(f32) * conv_ref[pos, :]).astype(bf16)

        qm = qm_ref[0, sub, :]
        zero = jnp.zeros_like(qm)
        lhs = jnp.concatenate([jnp.where(head_of_lane == j, qm, zero) for j in range(MX_HEADS)], axis=0)
        sc = lax.dot_general(lhs, memk_ref[...], (((1,), (1,)), ((), ())), preferred_element_type=f32)
        m = jnp.max(sc, axis=-1, keepdims=True)
        p = jnp.exp2(sc - m)
        l = jnp.sum(p, axis=-1, keepdims=True)
        r = jnp.dot(p.astype(bf16), memv_ref[...], preferred_element_type=f32) / l
        o_mx = jnp.zeros((OUT_SUB, mxw), f32)
        for j in range(MX_HEADS):
            o_mx = jnp.where(head_of_lane == j, r[j * OUT_SUB:(j + 1) * OUT_SUB], o_mx)
        o_mx = (o_mx * sgm_ref[0, sub, :].astype(f32)).astype(bf16)

        oda = jnp.concatenate([oda_ref[0, j, sub, :] for j in range(DA_HEADS)], axis=-1)
        out = (jnp.dot(oda, wo_ref[0:512, :], preferred_element_type=f32)
               + jnp.dot(o_cv, wo_ref[512:768, :], preferred_element_type=f32)
               + jnp.dot(o_mx, wo_ref[768:1024, :], preferred_element_type=f32))
        y_ref[0, sub, :] = _layer_norm(res_ref[0, sub, :] + out, lng_ref[...], lnb_ref[...])


def _mix_out(res, oda, bg, qm, sgm, u, mem, wkv, conv_w, wo, ln_g, ln_b):
    B, S, D = res.shape
    rows = OUT_ROWS
    n_mem = mem.shape[1]
    const = lambda b, t: (0, 0)
    row_spec = lambda w: pl.BlockSpec((1, rows, w), lambda b, t: (b, t, 0))
    return pl.pallas_call(
        _mix_out_kernel,
        grid=(B, S // rows),
        in_specs=[row_spec(D), pl.BlockSpec((1, DA_HEADS, rows, LANE), lambda b, t: (b, 0, t, 0)),
                  row_spec(256), row_spec(256), row_spec(256),
                  pl.BlockSpec((1, S, 256), lambda b, t: (b, 0, 0)),
                  pl.BlockSpec((1, n_mem, D), lambda b, t: (b, 0, 0)),
                  pl.BlockSpec(wkv.shape, const),
                  pl.BlockSpec(conv_w.shape, const),
                  pl.BlockSpec(wo.shape, const),
                  pl.BlockSpec((1, D), const), pl.BlockSpec((1, D), const)],
        out_specs=row_spec(D),
        out_shape=jax.ShapeDtypeStruct((B, S, D), f32),
        scratch_shapes=[pltpu.VMEM((n_mem, MX_HEADS * MX_DH), bf16),
                        pltpu.VMEM((n_mem, MX_HEADS * MX_DH), bf16),
                        pltpu.VMEM((S, 256), f32),
                        pltpu.VMEM((S + 2 * CONV_PAD, 256), f32)],
        compiler_params=pltpu.CompilerParams(
            dimension_semantics=("parallel", "arbitrary"),
            vmem_limit_bytes=VMEM_LIMIT),
        name="mix_out",
    )(res, oda, bg, qm, sgm, u, mem, wkv, conv_w, wo, ln_g, ln_b)


def _rope_tables(seq_len):
    inv_freq = ROPE_THETA ** (-jnp.arange(0, ROT_DIM, 2, dtype=f32) / ROT_DIM)
    ang = jnp.arange(seq_len, dtype=f32)[:, None] * inv_freq[None, :]
    cos, sin = jnp.cos(ang), jnp.sin(ang)
    half = ROT_DIM // 2
    pad = DA_DH - ROT_DIM
    ones = jnp.ones((seq_len, pad), f32)
    zeros_h = jnp.zeros((seq_len, half), f32)
    zeros_p = jnp.zeros((seq_len, pad), f32)
    c64 = jnp.concatenate([cos, cos, ones], axis=-1)
    sp64 = jnp.concatenate([zeros_h, sin, zeros_p], axis=-1)
    sm64 = jnp.concatenate([-sin, zeros_h, zeros_p], axis=-1)
    tile = lambda a: jnp.concatenate([a, a], axis=-1)
    return tile(c64), tile(sp64), tile(sm64)


def _trunk(x, mem, in_g, in_b, w_in, wkv, lq1, lk1, lq2, lk2, subln_g, conv_w, wo, ln_g, ln_b, tables):
    q, k, v, sg, u, bg, qm, sgm, res = _in_proj(x, in_g, in_b, w_in, *tables)
    oda = _diff_attn(q, k, v, sg, lq1, lk1, lq2, lk2, subln_g)
    return _mix_out(res, oda, bg, qm, sgm, u, mem, wkv, conv_w, wo, ln_g, ln_b)


def kernel(x_prompt, x_sample, mem_prompt, mem_sample, in_ln_g, in_ln_b, w_in, w_mem_kv, lam_q1, lam_k1, lam_q2, lam_k2, subln_g, conv_w, w_o, ln_g, ln_b):
    assert w_in.shape[0] == 1, "single-layer trunk"
    assert x_prompt.shape[1] == x_sample.shape[1]
    tables = _rope_tables(x_prompt.shape[1])
    params = (in_ln_g[None, :], in_ln_b[None, :], w_in[0].astype(bf16), w_mem_kv[0].astype(bf16),
              lam_q1, lam_k1, lam_q2, lam_k2, subln_g, conv_w[0], w_o[0].astype(bf16), ln_g, ln_b)
    y_prompt = _trunk(x_prompt, mem_prompt, *params, tables)
    y_sample = _trunk(x_sample, mem_sample, *params, tables)
    return (y_prompt, y_sample)
```

```python
import math

import jax
import jax.numpy as jnp
from jax import lax
from jax.experimental import pallas as pl
from jax.experimental.pallas import tpu as pltpu

LANE = 128
DA_HEADS = 4
DA_DH = 64
DA_DV = 128
MX_HEADS = 4
MX_DH = 64
ROT_DIM = 16
ROPE_THETA = 500000.0
LN_EPS = 1e-5
LAM_INIT = 0.8 - 0.6 * math.exp(-0.3 * 0)
ALPHA = 2.0 ** 0.25
LOG2E = math.log2(math.e)

PROJ_ROWS = 1024
PROJ_SUB = 512
ATTN_ROWS = 512
SCORE_BUFFERS = 3
FINISH_ROWS = 32
KEY_TILE = 256
OUT_ROWS = 1024
OUT_SUB = 512
CONV_ROWS = 64
CONV_PAD = 8
VMEM_LIMIT = 56 * 1024 * 1024

f32 = jnp.float32
bf16 = jnp.bfloat16


def _layer_norm(x, g, b):
    mu = jnp.mean(x, axis=-1, keepdims=True)
    xc = x - mu
    var = jnp.mean(xc * xc, axis=-1, keepdims=True)
    return xc * lax.rsqrt(var + LN_EPS) * g + b


def _silu(x):
    return x / (1.0 + jnp.exp(-x))


def _in_proj_kernel(x_ref, g_ref, b_ref, w_ref, cos_ref, sinp_ref, sinm_ref,
                    q_ref, k_ref, v_ref, sg_ref, u_ref, bg_ref, qm_ref, sgm_ref, res_ref):
    t = pl.program_id(1)
    rows = x_ref.shape[1]

    def rope(c, cos, sinp, sinm):
        return c * cos + pltpu.roll(c, 8, 1) * sinp + pltpu.roll(c, LANE - 8, 1) * sinm

    for s in range(rows // PROJ_SUB):
        sub = pl.ds(s * PROJ_SUB, PROJ_SUB)
        h32 = _layer_norm(x_ref[0, sub, :], g_ref[...], b_ref[...])
        res_ref[0, sub, :] = ALPHA * h32
        h = h32.astype(bf16)

        pos = pl.ds(pl.multiple_of(t * rows + s * PROJ_SUB, PROJ_SUB), PROJ_SUB)
        tabs = (cos_ref[pos, :], sinp_ref[pos, :], sinm_ref[pos, :])

        qk = jnp.dot(h, w_ref[:, 0:1024], preferred_element_type=f32)
        for j in range(DA_HEADS):
            q_ref[0, j, sub, :] = (rope(qk[:, j * LANE:(j + 1) * LANE], *tabs) * (DA_DH ** -0.5 * LOG2E)).astype(bf16)
            k_ref[0, j, sub, :] = rope(qk[:, 512 + j * LANE:512 + (j + 1) * LANE], *tabs).astype(bf16)

        vg = jnp.dot(h, w_ref[:, 1024:2048], preferred_element_type=f32)
        for j in range(DA_HEADS):
            v_ref[0, j, sub, :] = vg[:, j * LANE:(j + 1) * LANE].astype(bf16)
            sg_ref[0, j, sub, :] = _silu(vg[:, 512 + j * LANE:512 + (j + 1) * LANE]).astype(bf16)

        cv = jnp.dot(h, w_ref[:, 2048:3072], preferred_element_type=f32)
        u_ref[0, sub, :] = cv[:, 256:512] * cv[:, 512:768]
        bg_ref[0, sub, :] = (cv[:, 0:256] * _silu(cv[:, 768:1024])).astype(bf16)

        mx = jnp.dot(h, w_ref[:, 3072:3584], preferred_element_type=f32)
        qm_ref[0, sub, :] = (mx[:, 0:256] * (MX_DH ** -0.5 * LOG2E)).astype(bf16)
        sgm_ref[0, sub, :] = _silu(mx[:, 256:512]).astype(bf16)


def _in_proj(x, in_g, in_b, w_in, cos, sinp, sinm):
    B, S, D = x.shape
    rows = PROJ_ROWS
    grid = (B, S // rows)
    const = lambda b, t: (0, 0)
    resident = lambda shape: pl.BlockSpec(shape, const, pipeline_mode=pl.Buffered(1))
    head_spec = pl.BlockSpec((1, DA_HEADS, rows, LANE), lambda b, t: (b, 0, t, 0))
    row_spec = lambda w: pl.BlockSpec((1, rows, w), lambda b, t: (b, t, 0))
    head_shape = jax.ShapeDtypeStruct((B, DA_HEADS, S, LANE), bf16)
    return pl.pallas_call(
        _in_proj_kernel,
        grid=grid,
        in_specs=[
            pl.BlockSpec((1, rows, D), lambda b, t: (b, t, 0)),
            resident((1, D)), resident((1, D)), resident(w_in.shape),
            resident((S, LANE)), resident((S, LANE)), resident((S, LANE)),
        ],
        out_specs=[head_spec, head_spec, head_spec, head_spec, row_spec(256),
                   row_spec(256), row_spec(256), row_spec(256), row_spec(D)],
        out_shape=[head_shape, head_shape, head_shape, head_shape,
                   jax.ShapeDtypeStruct((B, S, 256), f32),
                   jax.ShapeDtypeStruct((B, S, 256), bf16),
                   jax.ShapeDtypeStruct((B, S, 256), bf16),
                   jax.ShapeDtypeStruct((B, S, 256), bf16),
                   jax.ShapeDtypeStruct((B, S, D), f32)],
        compiler_params=pltpu.CompilerParams(
            dimension_semantics=("parallel", "arbitrary"),
            vmem_limit_bytes=VMEM_LIMIT),
        name="in_proj",
    )(x, in_g, in_b, w_in, cos, sinp, sinm)


def _diff_attn_kernel(q_ref, k_ref, v_ref, sg_ref, lq1_ref, lk1_ref, lq2_ref, lk2_ref, sub_ref,
                      o_ref, vext_ref, *score_refs):
    bufs = list(zip(score_refs[:SCORE_BUFFERS], score_refs[SCORE_BUFFERS:]))
    H, S = k_ref.shape[1], k_ref.shape[2]
    tq = ATTN_ROWS
    tiles_per_head = S // tq
    n_tiles = H * tiles_per_head
    for h in range(H):
        vext_ref[h, :, 0:DA_DV] = v_ref[0, h]
        vext_ref[h, :, DA_DV:2 * DA_DV] = jnp.ones((S, DA_DV), bf16)

    lam = (jnp.exp(jnp.sum(lq1_ref[...] * lk1_ref[...], axis=-1, keepdims=True))
           - jnp.exp(jnp.sum(lq2_ref[...] * lk2_ref[...], axis=-1, keepdims=True)) + LAM_INIT)
    gain = sub_ref[...] * (1.0 - LAM_INIT)
    first_map = lax.broadcasted_iota(jnp.int32, (tq, LANE), 1) < DA_DH

    def tile_pos(n):
        return n // tiles_per_head, pl.multiple_of((n % tiles_per_head) * tq, tq)

    key_tiles = S // KEY_TILE

    def scores_and_row_max(n, s_ref, m_ref):
        h, r0 = tile_pos(n)
        qh = q_ref[0, h, pl.ds(r0, tq), :]
        zero = jnp.zeros_like(qh)
        lhs = jnp.concatenate([jnp.where(first_map, qh, zero), jnp.where(first_map, zero, qh)], axis=0)
        m = None
        for j in range(key_tiles):
            keys = pl.ds(j * KEY_TILE, KEY_TILE)
            s = lax.dot_general(lhs, k_ref[0, h, keys, :], (((1,), (1,)), ((), ())), preferred_element_type=f32)
            s_ref[:, keys] = s
            for i in range(KEY_TILE // LANE):
                part = s[:, i * LANE:(i + 1) * LANE]
                m = part if m is None else jnp.maximum(m, part)
        m = jnp.max(m, axis=-1, keepdims=True)
        m_ref[...] = jnp.broadcast_to(m, (2 * tq, LANE))

    def weighted_values(n, s_ref, m_ref):
        h, r0 = tile_pos(n)
        pv = None
        for j in range(key_tiles):
            halves = [jnp.exp2(s_ref[:, pl.ds(j * KEY_TILE + i * LANE, LANE)] - m_ref[...]).astype(bf16)
                      for i in range(KEY_TILE // LANE)]
            part = jnp.dot(jnp.concatenate(halves, axis=1), vext_ref[h, pl.ds(j * KEY_TILE, KEY_TILE), :],
                           preferred_element_type=f32)
            pv = part if pv is None else pv + part
        for c in range(tq // FINISH_ROWS):
            a, b = c * FINISH_ROWS, (c + 1) * FINISH_ROWS
            o = pv[a:b, :DA_DV] / pv[a:b, DA_DV:] - lam * (pv[tq + a:tq + b, :DA_DV] / pv[tq + a:tq + b, DA_DV:])
            o = o * lax.rsqrt(jnp.mean(o * o, axis=-1, keepdims=True) + LN_EPS) * gain
            rows = pl.ds(r0 + a, FINISH_ROWS)
            o_ref[0, h, rows, :] = (o * sg_ref[0, h, rows, :].astype(f32)).astype(bf16)

    scores_and_row_max(0, *bufs[0])

    def steady(i, carry):
        n = len(bufs) * i + 1
        for d in range(len(bufs)):
            scores_and_row_max(n + d, *bufs[(1 + d) % len(bufs)])
            weighted_values(n + d - 1, *bufs[d % len(bufs)])
        return carry

    assert (n_tiles - 1) % len(bufs) == 0
    lax.fori_loop(0, (n_tiles - 1) // len(bufs), steady, 0)
    weighted_values(n_tiles - 1, *bufs[(n_tiles - 1) % len(bufs)])


def _diff_attn(q, k, v, sg, lq1, lk1, lq2, lk2, subln_g):
    B, H, S, _ = q.shape
    head_spec = pl.BlockSpec((1, H, S, LANE), lambda b: (b, 0, 0, 0))
    const = lambda b: (0, 0)
    return pl.pallas_call(
        _diff_attn_kernel,
        grid=(B,),
        in_specs=[head_spec, head_spec, head_spec, head_spec,
                  pl.BlockSpec((1, DA_DH), const), pl.BlockSpec((1, DA_DH), const),
                  pl.BlockSpec((1, DA_DH), const), pl.BlockSpec((1, DA_DH), const),
                  pl.BlockSpec((1, DA_DV), const)],
        out_specs=head_spec,
        out_shape=jax.ShapeDtypeStruct((B, H, S, DA_DV), bf16),
        scratch_shapes=[pltpu.VMEM((H, S, 2 * DA_DV), bf16),
                        *[pltpu.VMEM((2 * ATTN_ROWS, S), f32) for _ in range(SCORE_BUFFERS)],
                        *[pltpu.VMEM((2 * ATTN_ROWS, LANE), f32) for _ in range(SCORE_BUFFERS)]],
        compiler_params=pltpu.CompilerParams(
            dimension_semantics=("parallel",),
            vmem_limit_bytes=VMEM_LIMIT),
        name="diff_attn",
    )(q, k, v, sg, lq1, lk1, lq2, lk2, subln_g)


def _mix_out_kernel(res_ref, oda_ref, bg_ref, qm_ref, sgm_ref, u_ref, mem_ref, wkv_ref, cw_ref, wo_ref,
                    lng_ref, lnb_ref, y_ref, memk_ref, memv_ref, upad_ref):
    t = pl.program_id(1)
    rows = res_ref.shape[1]
    S = u_ref.shape[1]
    mxw = MX_HEADS * MX_DH

    @pl.when(t == 0)
    def _():
        kv = jnp.dot(mem_ref[0].astype(bf16), wkv_ref[...], preferred_element_type=f32)
        memk_ref[...] = kv[:, :mxw].astype(bf16)
        memv_ref[...] = kv[:, mxw:].astype(bf16)
        pad = jnp.zeros((CONV_PAD, u_ref.shape[2]), f32)
        upad_ref[0:CONV_PAD, :] = pad
        upad_ref[CONV_PAD + S:2 * CONV_PAD + S, :] = pad
        upad_ref[CONV_PAD:CONV_PAD + S, :] = u_ref[0]

    lane = lax.broadcasted_iota(jnp.int32, (OUT_SUB, mxw), 1)
    head_of_lane = lane // MX_DH

    def gated_conv(row0, bg_rows):
        w = upad_ref[pl.ds(pl.multiple_of(row0, CONV_ROWS), CONV_ROWS + 2 * CONV_PAD), :]
        shifted = [pltpu.roll(w, 1, 0), w, pltpu.roll(w, w.shape[0] - 1, 0)]
        taps = [shifted[k][CONV_PAD:CONV_PAD + CONV_ROWS, :] * cw_ref[k:k + 1, :] for k in range(3)]
        return (bg_ref[0, bg_rows, :].astype(f32) * (taps[0] + taps[1] + taps[2])).astype(bf16)

    def gated_conv_tile(s):
        return jnp.concatenate(
            [gated_conv(t * rows + s * OUT_SUB + c * CONV_ROWS, pl.ds(s * OUT_SUB + c * CONV_ROWS, CONV_ROWS))
             for c in range(OUT_SUB // CONV_ROWS)], axis=0)

    def gated_memory(s):
        sub = pl.ds(s * OUT_SUB, OUT_SUB)
        qm = qm_ref[0, sub, :]
        zero = jnp.zeros_like(qm)
        lhs = jnp.concatenate([jnp.where(head_of_lane == j, qm, zero) for j in range(MX_HEADS)], axis=0)
        sc = lax.dot_general(lhs, memk_ref[...], (((1,), (1,)), ((), ())), preferred_element_type=f32)
        m = jnp.max(sc, axis=-1, keepdims=True)
        p = jnp.exp2(sc - m)
        l = jnp.sum(p, axis=-1, keepdims=True)
        r = jnp.dot(p.astype(bf16), memv_ref[...], preferred_element_type=f32) / l
        o_mx = jnp.zeros((OUT_SUB, mxw), f32)
        for j in range(MX_HEADS):
            o_mx = jnp.where(head_of_lane == j, r[j * OUT_SUB:(j + 1) * OUT_SUB], o_mx)
        return (o_mx * sgm_ref[0, sub, :].astype(f32)).astype(bf16)

    def attention_part(s):
        oda = jnp.concatenate([oda_ref[0, j, pl.ds(s * OUT_SUB, OUT_SUB), :] for j in range(DA_HEADS)], axis=-1)
        return jnp.dot(oda, wo_ref[0:512, :], preferred_element_type=f32)

    subs = range(rows // OUT_SUB)
    attn = [attention_part(s) for s in subs]
    memory = [gated_memory(s) for s in subs]
    conv = [gated_conv_tile(s) for s in subs]
    mixed = list(zip(conv, memory))
    for s, (o_cv, o_mx) in enumerate(mixed):
        sub = pl.ds(s * OUT_SUB, OUT_SUB)
        out = (attn[s]
               + jnp.dot(o_cv, wo_ref[512:768, :], preferred_element_type=f32)
               + jnp.dot(o_mx, wo_ref[768:1024, :], preferred_element_type=f32))
        y_ref[0, sub, :] = _layer_norm(res_ref[0, sub, :] + out, lng_ref[...], lnb_ref[...])


def _mix_out(res, oda, bg, qm, sgm, u, mem, wkv, conv_w, wo, ln_g, ln_b):
    B, S, D = res.shape
    rows = OUT_ROWS
    n_mem = mem.shape[1]
    const = lambda b, t: (0, 0)
    row_spec = lambda w: pl.BlockSpec((1, rows, w), lambda b, t: (b, t, 0))
    return pl.pallas_call(
        _mix_out_kernel,
        grid=(B, S // rows),
        in_specs=[row_spec(D), pl.BlockSpec((1, DA_HEADS, rows, LANE), lambda b, t: (b, 0, t, 0)),
                  row_spec(256), row_spec(256), row_spec(256),
                  pl.BlockSpec((1, S, 256), lambda b, t: (b, 0, 0)),
                  pl.BlockSpec((1, n_mem, D), lambda b, t: (b, 0, 0)),
                  pl.BlockSpec(wkv.shape, const),
                  pl.BlockSpec(conv_w.shape, const),
                  pl.BlockSpec(wo.shape, const),
                  pl.BlockSpec((1, D), const), pl.BlockSpec((1, D), const)],
        out_specs=row_spec(D),
        out_shape=jax.ShapeDtypeStruct((B, S, D), f32),
        scratch_shapes=[pltpu.VMEM((n_mem, MX_HEADS * MX_DH), bf16),
                        pltpu.VMEM((n_mem, MX_HEADS * MX_DH), bf16),
                        pltpu.VMEM((S + 2 * CONV_PAD, 256), f32)],
        compiler_params=pltpu.CompilerParams(
            dimension_semantics=("parallel", "arbitrary"),
            vmem_limit_bytes=VMEM_LIMIT),
        name="mix_out",
    )(res, oda, bg, qm, sgm, u, mem, wkv, conv_w, wo, ln_g, ln_b)


def _rope_tables(seq_len):
    inv_freq = ROPE_THETA ** (-jnp.arange(0, ROT_DIM, 2, dtype=f32) / ROT_DIM)
    ang = jnp.arange(seq_len, dtype=f32)[:, None] * inv_freq[None, :]
    cos, sin = jnp.cos(ang), jnp.sin(ang)
    half = ROT_DIM // 2
    pad = DA_DH - ROT_DIM
    ones = jnp.ones((seq_len, pad), f32)
    zeros_h = jnp.zeros((seq_len, half), f32)
    zeros_p = jnp.zeros((seq_len, pad), f32)
    c64 = jnp.concatenate([cos, cos, ones], axis=-1)
    sp64 = jnp.concatenate([zeros_h, sin, zeros_p], axis=-1)
    sm64 = jnp.concatenate([-sin, zeros_h, zeros_p], axis=-1)
    tile = lambda a: jnp.concatenate([a, a], axis=-1)
    return tile(c64), tile(sp64), tile(sm64)


def _trunk(x, mem, in_g, in_b, w_in, wkv, lq1, lk1, lq2, lk2, subln_g, conv_w, wo, ln_g, ln_b, tables):
    q, k, v, sg, u, bg, qm, sgm, res = _in_proj(x, in_g, in_b, w_in, *tables)
    oda = _diff_attn(q, k, v, sg, lq1, lk1, lq2, lk2, subln_g)
    return _mix_out(res, oda, bg, qm, sgm, u, mem, wkv, conv_w, wo, ln_g, ln_b)


def kernel(x_prompt, x_sample, mem_prompt, mem_sample, in_ln_g, in_ln_b, w_in, w_mem_kv, lam_q1, lam_k1, lam_q2, lam_k2, subln_g, conv_w, w_o, ln_g, ln_b):
    assert w_in.shape[0] == 1, "single-layer trunk"
    assert x_prompt.shape[1] == x_sample.shape[1]
    tables = _rope_tables(x_prompt.shape[1])
    params = (in_ln_g[None, :], in_ln_b[None, :], w_in[0].astype(bf16), w_mem_kv[0].astype(bf16),
              lam_q1, lam_k1, lam_q2, lam_k2, subln_g, conv_w[0], w_o[0].astype(bf16), ln_g, ln_b)
    y_prompt = _trunk(x_prompt, mem_prompt, *params, tables)
    y_sample = _trunk(x_sample, mem_sample, *params, tables)
    return (y_prompt, y_sample)
```

```python
import math

import jax
import jax.numpy as jnp
from jax import lax
from jax.experimental import pallas as pl
from jax.experimental.pallas import tpu as pltpu

LANE = 128
DA_HEADS = 4
DA_DH = 64
DA_DV = 128
MX_HEADS = 4
MX_DH = 64
ROT_DIM = 16
ROPE_THETA = 500000.0
LN_EPS = 1e-5
LAM_INIT = 0.8 - 0.6 * math.exp(-0.3 * 0)
ALPHA = 2.0 ** 0.25
LOG2E = math.log2(math.e)

PROJ_ROWS = 1024
PROJ_SUB = 512
ATTN_ROWS = 512
SCORE_BUFFERS = 3
STEPS_PER_TRIP = 6
FINISH_ROWS = 32
KEY_TILE = 256
OUT_ROWS = 1024
OUT_SUB = 512
CONV_ROWS = 64
CONV_PAD = 8
VMEM_LIMIT = 56 * 1024 * 1024

f32 = jnp.float32
bf16 = jnp.bfloat16


def _layer_norm(x, g, b):
    mu = jnp.mean(x, axis=-1, keepdims=True)
    xc = x - mu
    var = jnp.mean(xc * xc, axis=-1, keepdims=True)
    return xc * lax.rsqrt(var + LN_EPS) * g + b


def _silu(x):
    return x / (1.0 + jnp.exp(-x))


def _in_proj_kernel(x_ref, g_ref, b_ref, w_ref, cos_ref, sinp_ref, sinm_ref,
                    q_ref, k_ref, v_ref, sg_ref, u_ref, bg_ref, qm_ref, sgm_ref, res_ref):
    t = pl.program_id(1)
    rows = x_ref.shape[1]

    def rope(c, cos, sinp, sinm):
        return c * cos + pltpu.roll(c, 8, 1) * sinp + pltpu.roll(c, LANE - 8, 1) * sinm

    for s in range(rows // PROJ_SUB):
        sub = pl.ds(s * PROJ_SUB, PROJ_SUB)
        h32 = _layer_norm(x_ref[0, sub, :], g_ref[...], b_ref[...])
        res_ref[0, sub, :] = ALPHA * h32
        h = h32.astype(bf16)

        pos = pl.ds(pl.multiple_of(t * rows + s * PROJ_SUB, PROJ_SUB), PROJ_SUB)
        tabs = (cos_ref[pos, :], sinp_ref[pos, :], sinm_ref[pos, :])

        qk = jnp.dot(h, w_ref[:, 0:1024], preferred_element_type=f32)
        for j in range(DA_HEADS):
            q_ref[0, j, sub, :] = (rope(qk[:, j * LANE:(j + 1) * LANE], *tabs) * (DA_DH ** -0.5 * LOG2E)).astype(bf16)
            k_ref[0, j, sub, :] = rope(qk[:, 512 + j * LANE:512 + (j + 1) * LANE], *tabs).astype(bf16)

        vg = jnp.dot(h, w_ref[:, 1024:2048], preferred_element_type=f32)
        for j in range(DA_HEADS):
            v_ref[0, j, sub, :] = vg[:, j * LANE:(j + 1) * LANE].astype(bf16)
            sg_ref[0, j, sub, :] = _silu(vg[:, 512 + j * LANE:512 + (j + 1) * LANE]).astype(bf16)

        cv = jnp.dot(h, w_ref[:, 2048:3072], preferred_element_type=f32)
        u_ref[0, sub, :] = cv[:, 256:512] * cv[:, 512:768]
        bg_ref[0, sub, :] = (cv[:, 0:256] * _silu(cv[:, 768:1024])).astype(bf16)

        mx = jnp.dot(h, w_ref[:, 3072:3584], preferred_element_type=f32)
        qm_ref[0, sub, :] = (mx[:, 0:256] * (MX_DH ** -0.5 * LOG2E)).astype(bf16)
        sgm_ref[0, sub, :] = _silu(mx[:, 256:512]).astype(bf16)


def _in_proj(x, in_g, in_b, w_in, cos, sinp, sinm):
    B, S, D = x.shape
    rows = PROJ_ROWS
    grid = (B, S // rows)
    const = lambda b, t: (0, 0)
    resident = lambda shape: pl.BlockSpec(shape, const, pipeline_mode=pl.Buffered(1))
    head_spec = pl.BlockSpec((1, DA_HEADS, rows, LANE), lambda b, t: (b, 0, t, 0))
    row_spec = lambda w: pl.BlockSpec((1, rows, w), lambda b, t: (b, t, 0))
    head_shape = jax.ShapeDtypeStruct((B, DA_HEADS, S, LANE), bf16)
    return pl.pallas_call(
        _in_proj_kernel,
        grid=grid,
        in_specs=[
            pl.BlockSpec((1, rows, D), lambda b, t: (b, t, 0)),
            resident((1, D)), resident((1, D)), resident(w_in.shape),
            resident((S, LANE)), resident((S, LANE)), resident((S, LANE)),
        ],
        out_specs=[head_spec, head_spec, head_spec, head_spec, row_spec(256),
                   row_spec(256), row_spec(256), row_spec(256), row_spec(D)],
        out_shape=[head_shape, head_shape, head_shape, head_shape,
                   jax.ShapeDtypeStruct((B, S, 256), f32),
                   jax.ShapeDtypeStruct((B, S, 256), bf16),
                   jax.ShapeDtypeStruct((B, S, 256), bf16),
                   jax.ShapeDtypeStruct((B, S, 256), bf16),
                   jax.ShapeDtypeStruct((B, S, D), f32)],
        compiler_params=pltpu.CompilerParams(
            dimension_semantics=("parallel", "arbitrary"),
            vmem_limit_bytes=VMEM_LIMIT),
        name="in_proj",
    )(x, in_g, in_b, w_in, cos, sinp, sinm)


def _diff_attn_kernel(q_ref, k_ref, v_ref, sg_ref, lq1_ref, lk1_ref, lq2_ref, lk2_ref, sub_ref,
                      o_ref, vext_ref, *score_refs):
    bufs = list(zip(score_refs[:SCORE_BUFFERS], score_refs[SCORE_BUFFERS:]))
    H, S = k_ref.shape[1], k_ref.shape[2]
    tq = ATTN_ROWS
    tiles_per_head = S // tq
    n_tiles = H * tiles_per_head
    for h in range(H):
        vext_ref[h, :, 0:DA_DV] = v_ref[0, h]
        vext_ref[h, :, DA_DV:2 * DA_DV] = jnp.ones((S, DA_DV), bf16)

    lam = (jnp.exp(jnp.sum(lq1_ref[...] * lk1_ref[...], axis=-1, keepdims=True))
           - jnp.exp(jnp.sum(lq2_ref[...] * lk2_ref[...], axis=-1, keepdims=True)) + LAM_INIT)
    gain = sub_ref[...] * (1.0 - LAM_INIT)
    first_map = lax.broadcasted_iota(jnp.int32, (tq, LANE), 1) < DA_DH

    def tile_pos(n):
        return n // tiles_per_head, pl.multiple_of((n % tiles_per_head) * tq, tq)

    key_tiles = S // KEY_TILE

    def scores_and_row_max(n, s_ref, m_ref):
        h, r0 = tile_pos(n)
        qh = q_ref[0, h, pl.ds(r0, tq), :]
        zero = jnp.zeros_like(qh)
        lhs = jnp.concatenate([jnp.where(first_map, qh, zero), jnp.where(first_map, zero, qh)], axis=0)
        m = None
        for j in range(key_tiles):
            keys = pl.ds(j * KEY_TILE, KEY_TILE)
            s = lax.dot_general(lhs, k_ref[0, h, keys, :], (((1,), (1,)), ((), ())), preferred_element_type=f32)
            s_ref[:, keys] = s
            for i in range(KEY_TILE // LANE):
                part = s[:, i * LANE:(i + 1) * LANE]
                m = part if m is None else jnp.maximum(m, part)
        m = jnp.max(m, axis=-1, keepdims=True)
        m_ref[...] = jnp.broadcast_to(m, (2 * tq, LANE))

    def weighted_values(n, s_ref, m_ref):
        h, r0 = tile_pos(n)
        pv = None
        for j in range(key_tiles):
            halves = [jnp.exp2(s_ref[:, pl.ds(j * KEY_TILE + i * LANE, LANE)] - m_ref[...]).astype(bf16)
                      for i in range(KEY_TILE // LANE)]
            part = jnp.dot(jnp.concatenate(halves, axis=1), vext_ref[h, pl.ds(j * KEY_TILE, KEY_TILE), :],
                           preferred_element_type=f32)
            pv = part if pv is None else pv + part
        for c in range(tq // FINISH_ROWS):
            a, b = c * FINISH_ROWS, (c + 1) * FINISH_ROWS
            o = pv[a:b, :DA_DV] / pv[a:b, DA_DV:] - lam * (pv[tq + a:tq + b, :DA_DV] / pv[tq + a:tq + b, DA_DV:])
            o = o * lax.rsqrt(jnp.mean(o * o, axis=-1, keepdims=True) + LN_EPS) * gain
            rows = pl.ds(r0 + a, FINISH_ROWS)
            o_ref[0, h, rows, :] = (o * sg_ref[0, h, rows, :].astype(f32)).astype(bf16)

    scores_and_row_max(0, *bufs[0])

    def pipeline_steps(first, count):
        for d in range(count):
            scores_and_row_max(first + d, *bufs[(1 + d) % len(bufs)])
            weighted_values(first + d - 1, *bufs[d % len(bufs)])

    def steady(i, carry):
        pipeline_steps(STEPS_PER_TRIP * i + 1, STEPS_PER_TRIP)
        return carry

    assert STEPS_PER_TRIP % len(bufs) == 0 and (n_tiles - 1) % len(bufs) == 0
    trips = (n_tiles - 1) // STEPS_PER_TRIP
    lax.fori_loop(0, trips, steady, 0)
    pipeline_steps(trips * STEPS_PER_TRIP + 1, n_tiles - 1 - trips * STEPS_PER_TRIP)
    weighted_values(n_tiles - 1, *bufs[(n_tiles - 1) % len(bufs)])


def _diff_attn(q, k, v, sg, lq1, lk1, lq2, lk2, subln_g):
    B, H, S, _ = q.shape
    head_spec = pl.BlockSpec((1, H, S, LANE), lambda b: (b, 0, 0, 0))
    const = lambda b: (0, 0)
    return pl.pallas_call(
        _diff_attn_kernel,
        grid=(B,),
        in_specs=[head_spec, head_spec, head_spec, head_spec,
                  pl.BlockSpec((1, DA_DH), const), pl.BlockSpec((1, DA_DH), const),
                  pl.BlockSpec((1, DA_DH), const), pl.BlockSpec((1, DA_DH), const),
                  pl.BlockSpec((1, DA_DV), const)],
        out_specs=head_spec,
        out_shape=jax.ShapeDtypeStruct((B, H, S, DA_DV), bf16),
        scratch_shapes=[pltpu.VMEM((H, S, 2 * DA_DV), bf16),
                        *[pltpu.VMEM((2 * ATTN_ROWS, S), f32) for _ in range(SCORE_BUFFERS)],
                        *[pltpu.VMEM((2 * ATTN_ROWS, LANE), f32) for _ in range(SCORE_BUFFERS)]],
        compiler_params=pltpu.CompilerParams(
            dimension_semantics=("parallel",),
            vmem_limit_bytes=VMEM_LIMIT),
        name="diff_attn",
    )(q, k, v, sg, lq1, lk1, lq2, lk2, subln_g)


def _mix_out_kernel(res_ref, oda_ref, bg_ref, qm_ref, sgm_ref, u_ref, mem_ref, wkv_ref, cw_ref, wo_ref,
                    lng_ref, lnb_ref, y_ref, memk_ref, memv_ref, upad_ref):
    t = pl.program_id(1)
    rows = res_ref.shape[1]
    S = u_ref.shape[1]
    mxw = MX_HEADS * MX_DH

    @pl.when(t == 0)
    def _():
        kv = jnp.dot(mem_ref[0].astype(bf16), wkv_ref[...], preferred_element_type=f32)
        memk_ref[...] = kv[:, :mxw].astype(bf16)
        memv_ref[...] = kv[:, mxw:].astype(bf16)
        pad = jnp.zeros((CONV_PAD, u_ref.shape[2]), f32)
        upad_ref[0:CONV_PAD, :] = pad
        upad_ref[CONV_PAD + S:2 * CONV_PAD + S, :] = pad
        upad_ref[CONV_PAD:CONV_PAD + S, :] = u_ref[0]

    lane = lax.broadcasted_iota(jnp.int32, (OUT_SUB, mxw), 1)
    head_of_lane = lane // MX_DH

    def gated_conv(row0, bg_rows):
        w = upad_ref[pl.ds(pl.multiple_of(row0, CONV_ROWS), CONV_ROWS + 2 * CONV_PAD), :]
        shifted = [pltpu.roll(w, 1, 0), w, pltpu.roll(w, w.shape[0] - 1, 0)]
        taps = [shifted[k][CONV_PAD:CONV_PAD + CONV_ROWS, :] * cw_ref[k:k + 1, :] for k in range(3)]
        return (bg_ref[0, bg_rows, :].astype(f32) * (taps[0] + taps[1] + taps[2])).astype(bf16)

    def gated_conv_tile(s):
        return jnp.concatenate(
            [gated_conv(t * rows + s * OUT_SUB + c * CONV_ROWS, pl.ds(s * OUT_SUB + c * CONV_ROWS, CONV_ROWS))
             for c in range(OUT_SUB // CONV_ROWS)], axis=0)

    def gated_memory(s):
        sub = pl.ds(s * OUT_SUB, OUT_SUB)
        qm = qm_ref[0, sub, :]
        zero = jnp.zeros_like(qm)
        lhs = jnp.concatenate([jnp.where(head_of_lane == j, qm, zero) for j in range(MX_HEADS)], axis=0)
        sc = lax.dot_general(lhs, memk_ref[...], (((1,), (1,)), ((), ())), preferred_element_type=f32)
        m = jnp.max(sc, axis=-1, keepdims=True)
        p = jnp.exp2(sc - m)
        l = jnp.sum(p, axis=-1, keepdims=True)
        r = jnp.dot(p.astype(bf16), memv_ref[...], preferred_element_type=f32) / l
        o_mx = jnp.zeros((OUT_SUB, mxw), f32)
        for j in range(MX_HEADS):
            o_mx = jnp.where(head_of_lane == j, r[j * OUT_SUB:(j + 1) * OUT_SUB], o_mx)
        return (o_mx * sgm_ref[0, sub, :].astype(f32)).astype(bf16)

    def attention_part(s):
        oda = jnp.concatenate([oda_ref[0, j, pl.ds(s * OUT_SUB, OUT_SUB), :] for j in range(DA_HEADS)], axis=-1)
        return jnp.dot(oda, wo_ref[0:512, :], preferred_element_type=f32)

    subs = range(rows // OUT_SUB)
    attn = [attention_part(s) for s in subs]
    memory = [gated_memory(s) for s in subs]
    conv = [gated_conv_tile(s) for s in subs]
    mixed = list(zip(conv, memory))
    for s, (o_cv, o_mx) in enumerate(mixed):
        sub = pl.ds(s * OUT_SUB, OUT_SUB)
        out = (attn[s]
               + jnp.dot(o_cv, wo_ref[512:768, :], preferred_element_type=f32)
               + jnp.dot(o_mx, wo_ref[768:1024, :], preferred_element_type=f32))
        y_ref[0, sub, :] = _layer_norm(res_ref[0, sub, :] + out, lng_ref[...], lnb_ref[...])


def _mix_out(res, oda, bg, qm, sgm, u, mem, wkv, conv_w, wo, ln_g, ln_b):
    B, S, D = res.shape
    rows = OUT_ROWS
    n_mem = mem.shape[1]
    const = lambda b, t: (0, 0)
    row_spec = lambda w: pl.BlockSpec((1, rows, w), lambda b, t: (b, t, 0))
    return pl.pallas_call(
        _mix_out_kernel,
        grid=(B, S // rows),
        in_specs=[row_spec(D), pl.BlockSpec((1, DA_HEADS, rows, LANE), lambda b, t: (b, 0, t, 0)),
                  row_spec(256), row_spec(256), row_spec(256),
                  pl.BlockSpec((1, S, 256), lambda b, t: (b, 0, 0)),
                  pl.BlockSpec((1, n_mem, D), lambda b, t: (b, 0, 0)),
                  pl.BlockSpec(wkv.shape, const),
                  pl.BlockSpec(conv_w.shape, const),
                  pl.BlockSpec(wo.shape, const),
                  pl.BlockSpec((1, D), const), pl.BlockSpec((1, D), const)],
        out_specs=row_spec(D),
        out_shape=jax.ShapeDtypeStruct((B, S, D), f32),
        scratch_shapes=[pltpu.VMEM((n_mem, MX_HEADS * MX_DH), bf16),
                        pltpu.VMEM((n_mem, MX_HEADS * MX_DH), bf16),
                        pltpu.VMEM((S + 2 * CONV_PAD, 256), f32)],
        compiler_params=pltpu.CompilerParams(
            dimension_semantics=("parallel", "arbitrary"),
            vmem_limit_bytes=VMEM_LIMIT),
        name="mix_out",
    )(res, oda, bg, qm, sgm, u, mem, wkv, conv_w, wo, ln_g, ln_b)


def _rope_tables(seq_len):
    inv_freq = ROPE_THETA ** (-jnp.arange(0, ROT_DIM, 2, dtype=f32) / ROT_DIM)
    ang = jnp.arange(seq_len, dtype=f32)[:, None] * inv_freq[None, :]
    cos, sin = jnp.cos(ang), jnp.sin(ang)
    half = ROT_DIM // 2
    pad = DA_DH - ROT_DIM
    ones = jnp.ones((seq_len, pad), f32)
    zeros_h = jnp.zeros((seq_len, half), f32)
    zeros_p = jnp.zeros((seq_len, pad), f32)
    c64 = jnp.concatenate([cos, cos, ones], axis=-1)
    sp64 = jnp.concatenate([zeros_h, sin, zeros_p], axis=-1)
    sm64 = jnp.concatenate([-sin, zeros_h, zeros_p], axis=-1)
    tile = lambda a: jnp.concatenate([a, a], axis=-1)
    return tile(c64), tile(sp64), tile(sm64)


def _trunk(x, mem, in_g, in_b, w_in, wkv, lq1, lk1, lq2, lk2, subln_g, conv_w, wo, ln_g, ln_b, tables):
    q, k, v, sg, u, bg, qm, sgm, res = _in_proj(x, in_g, in_b, w_in, *tables)
    oda = _diff_attn(q, k, v, sg, lq1, lk1, lq2, lk2, subln_g)
    return _mix_out(res, oda, bg, qm, sgm, u, mem, wkv, conv_w, wo, ln_g, ln_b)


def kernel(x_prompt, x_sample, mem_prompt, mem_sample, in_ln_g, in_ln_b, w_in, w_mem_kv, lam_q1, lam_k1, lam_q2, lam_k2, subln_g, conv_w, w_o, ln_g, ln_b):
    assert w_in.shape[0] == 1, "single-layer trunk"
    assert x_prompt.shape[1] == x_sample.shape[1]
    tables = _rope_tables(x_prompt.shape[1])
    params = (in_ln_g[None, :], in_ln_b[None, :], w_in[0].astype(bf16), w_mem_kv[0].astype(bf16),
              lam_q1, lam_k1, lam_q2, lam_k2, subln_g, conv_w[0], w_o[0].astype(bf16), ln_g, ln_b)
    y_prompt = _trunk(x_prompt, mem_prompt, *params, tables)
    y_sample = _trunk(x_sample, mem_sample, *params, tables)
    return (y_prompt, y_sample)
```

```python
import math

import jax
import jax.numpy as jnp
from jax import lax
from jax.experimental import pallas as pl
from jax.experimental.pallas import tpu as pltpu

LANE = 128
DA_HEADS = 4
DA_DH = 64
DA_DV = 128
MX_HEADS = 4
MX_DH = 64
ROT_DIM = 16
ROPE_THETA = 500000.0
LN_EPS = 1e-5
LAM_INIT = 0.8 - 0.6 * math.exp(-0.3 * 0)
ALPHA = 2.0 ** 0.25
LOG2E = math.log2(math.e)

PROJ_ROWS = 1024
PROJ_SUB = 512
ATTN_ROWS = 512
SCORE_BUFFERS = 3
STEPS_PER_TRIP = 15
FINISH_ROWS = 32
KEY_TILE = 256
OUT_ROWS = 1024
OUT_SUB = 512
CONV_ROWS = 64
CONV_PAD = 8
VMEM_LIMIT = 56 * 1024 * 1024

f32 = jnp.float32
bf16 = jnp.bfloat16


def _layer_norm(x, g, b):
    mu = jnp.mean(x, axis=-1, keepdims=True)
    xc = x - mu
    var = jnp.mean(xc * xc, axis=-1, keepdims=True)
    return xc * lax.rsqrt(var + LN_EPS) * g + b


def _silu(x):
    return x / (1.0 + jnp.exp(-x))


def _in_proj_kernel(x_ref, g_ref, b_ref, w_ref, cos_ref, sinp_ref, sinm_ref,
                    q_ref, k_ref, v_ref, sg_ref, u_ref, bg_ref, qm_ref, sgm_ref, res_ref):
    t = pl.program_id(1)
    rows = x_ref.shape[1]

    def rope(c, cos, sinp, sinm):
        return c * cos + pltpu.roll(c, 8, 1) * sinp + pltpu.roll(c, LANE - 8, 1) * sinm

    for s in range(rows // PROJ_SUB):
        sub = pl.ds(s * PROJ_SUB, PROJ_SUB)
        h32 = _layer_norm(x_ref[0, sub, :], g_ref[...], b_ref[...])
        res_ref[0, sub, :] = ALPHA * h32
        h = h32.astype(bf16)

        pos = pl.ds(pl.multiple_of(t * rows + s * PROJ_SUB, PROJ_SUB), PROJ_SUB)
        tabs = (cos_ref[pos, :], sinp_ref[pos, :], sinm_ref[pos, :])

        qk = jnp.dot(h, w_ref[:, 0:1024], preferred_element_type=f32)
        for j in range(DA_HEADS):
            q_ref[0, j, sub, :] = (rope(qk[:, j * LANE:(j + 1) * LANE], *tabs) * (DA_DH ** -0.5 * LOG2E)).astype(bf16)
            k_ref[0, j, sub, :] = rope(qk[:, 512 + j * LANE:512 + (j + 1) * LANE], *tabs).astype(bf16)

        vg = jnp.dot(h, w_ref[:, 1024:2048], preferred_element_type=f32)
        for j in range(DA_HEADS):
            v_ref[0, j, sub, :] = vg[:, j * LANE:(j + 1) * LANE].astype(bf16)
            sg_ref[0, j, sub, :] = _silu(vg[:, 512 + j * LANE:512 + (j + 1) * LANE]).astype(bf16)

        cv = jnp.dot(h, w_ref[:, 2048:3072], preferred_element_type=f32)
        u_ref[0, sub, :] = cv[:, 256:512] * cv[:, 512:768]
        bg_ref[0, sub, :] = (cv[:, 0:256] * _silu(cv[:, 768:1024])).astype(bf16)

        mx = jnp.dot(h, w_ref[:, 3072:3584], preferred_element_type=f32)
        qm_ref[0, sub, :] = (mx[:, 0:256] * (MX_DH ** -0.5 * LOG2E)).astype(bf16)
        sgm_ref[0, sub, :] = _silu(mx[:, 256:512]).astype(bf16)


def _in_proj(x, in_g, in_b, w_in, cos, sinp, sinm):
    B, S, D = x.shape
    rows = PROJ_ROWS
    grid = (B, S // rows)
    const = lambda b, t: (0, 0)
    resident = lambda shape: pl.BlockSpec(shape, const, pipeline_mode=pl.Buffered(1))
    head_spec = pl.BlockSpec((1, DA_HEADS, rows, LANE), lambda b, t: (b, 0, t, 0))
    row_spec = lambda w: pl.BlockSpec((1, rows, w), lambda b, t: (b, t, 0))
    head_shape = jax.ShapeDtypeStruct((B, DA_HEADS, S, LANE), bf16)
    return pl.pallas_call(
        _in_proj_kernel,
        grid=grid,
        in_specs=[
            pl.BlockSpec((1, rows, D), lambda b, t: (b, t, 0)),
            resident((1, D)), resident((1, D)), resident(w_in.shape),
            resident((S, LANE)), resident((S, LANE)), resident((S, LANE)),
        ],
        out_specs=[head_spec, head_spec, head_spec, head_spec, row_spec(256),
                   row_spec(256), row_spec(256), row_spec(256), row_spec(D)],
        out_shape=[head_shape, head_shape, head_shape, head_shape,
                   jax.ShapeDtypeStruct((B, S, 256), f32),
                   jax.ShapeDtypeStruct((B, S, 256), bf16),
                   jax.ShapeDtypeStruct((B, S, 256), bf16),
                   jax.ShapeDtypeStruct((B, S, 256), bf16),
                   jax.ShapeDtypeStruct((B, S, D), f32)],
        compiler_params=pltpu.CompilerParams(
            dimension_semantics=("parallel", "arbitrary"),
            vmem_limit_bytes=VMEM_LIMIT),
        name="in_proj",
    )(x, in_g, in_b, w_in, cos, sinp, sinm)


def _diff_attn_kernel(q_ref, k_ref, v_ref, sg_ref, lq1_ref, lk1_ref, lq2_ref, lk2_ref, sub_ref,
                      o_ref, vext_ref, *score_refs):
    bufs = list(zip(score_refs[:SCORE_BUFFERS], score_refs[SCORE_BUFFERS:]))
    H, S = k_ref.shape[1], k_ref.shape[2]
    tq = ATTN_ROWS
    tiles_per_head = S // tq
    n_tiles = H * tiles_per_head
    for h in range(H):
        vext_ref[h, :, 0:DA_DV] = v_ref[0, h]
        vext_ref[h, :, DA_DV:2 * DA_DV] = jnp.ones((S, DA_DV), bf16)

    lam = (jnp.exp(jnp.sum(lq1_ref[...] * lk1_ref[...], axis=-1, keepdims=True))
           - jnp.exp(jnp.sum(lq2_ref[...] * lk2_ref[...], axis=-1, keepdims=True)) + LAM_INIT)
    gain = sub_ref[...] * (1.0 - LAM_INIT)
    first_map = lax.broadcasted_iota(jnp.int32, (tq, LANE), 1) < DA_DH

    def tile_pos(n):
        return n // tiles_per_head, pl.multiple_of((n % tiles_per_head) * tq, tq)

    key_tiles = S // KEY_TILE

    def scores_and_row_max(n, s_ref, m_ref):
        h, r0 = tile_pos(n)
        qh = q_ref[0, h, pl.ds(r0, tq), :]
        zero = jnp.zeros_like(qh)
        lhs = jnp.concatenate([jnp.where(first_map, qh, zero), jnp.where(first_map, zero, qh)], axis=0)
        m = None
        for j in range(key_tiles):
            keys = pl.ds(j * KEY_TILE, KEY_TILE)
            s = lax.dot_general(lhs, k_ref[0, h, keys, :], (((1,), (1,)), ((), ())), preferred_element_type=f32)
            s_ref[:, keys] = s
            for i in range(KEY_TILE // LANE):
                part = s[:, i * LANE:(i + 1) * LANE]
                m = part if m is None else jnp.maximum(m, part)
        m = jnp.max(m, axis=-1, keepdims=True)
        m_ref[...] = jnp.broadcast_to(m, (2 * tq, LANE))

    def weighted_values(n, s_ref, m_ref):
        h, r0 = tile_pos(n)
        pv = None
        for j in range(key_tiles):
            halves = [jnp.exp2(s_ref[:, pl.ds(j * KEY_TILE + i * LANE, LANE)] - m_ref[...]).astype(bf16)
                      for i in range(KEY_TILE // LANE)]
            part = jnp.dot(jnp.concatenate(halves, axis=1), vext_ref[h, pl.ds(j * KEY_TILE, KEY_TILE), :],
                           preferred_element_type=f32)
            pv = part if pv is None else pv + part
        for c in range(tq // FINISH_ROWS):
            a, b = c * FINISH_ROWS, (c + 1) * FINISH_ROWS
            o = pv[a:b, :DA_DV] / pv[a:b, DA_DV:] - lam * (pv[tq + a:tq + b, :DA_DV] / pv[tq + a:tq + b, DA_DV:])
            o = o * lax.rsqrt(jnp.mean(o * o, axis=-1, keepdims=True) + LN_EPS) * gain
            rows = pl.ds(r0 + a, FINISH_ROWS)
            o_ref[0, h, rows, :] = (o * sg_ref[0, h, rows, :].astype(f32)).astype(bf16)

    scores_and_row_max(0, *bufs[0])

    def pipeline_steps(first, count):
        for d in range(count):
            scores_and_row_max(first + d, *bufs[(1 + d) % len(bufs)])
            weighted_values(first + d - 1, *bufs[d % len(bufs)])

    def steady(i, carry):
        pipeline_steps(STEPS_PER_TRIP * i + 1, STEPS_PER_TRIP)
        return carry

    assert STEPS_PER_TRIP % len(bufs) == 0 and (n_tiles - 1) % len(bufs) == 0
    trips = (n_tiles - 1) // STEPS_PER_TRIP
    lax.fori_loop(0, trips, steady, 0)
    pipeline_steps(trips * STEPS_PER_TRIP + 1, n_tiles - 1 - trips * STEPS_PER_TRIP)
    weighted_values(n_tiles - 1, *bufs[(n_tiles - 1) % len(bufs)])


def _diff_attn(q, k, v, sg, lq1, lk1, lq2, lk2, subln_g):
    B, H, S, _ = q.shape
    head_spec = pl.BlockSpec((1, H, S, LANE), lambda b: (b, 0, 0, 0))
    const = lambda b: (0, 0)
    return pl.pallas_call(
        _diff_attn_kernel,
        grid=(B,),
        in_specs=[head_spec, head_spec, head_spec, head_spec,
                  pl.BlockSpec((1, DA_DH), const), pl.BlockSpec((1, DA_DH), const),
                  pl.BlockSpec((1, DA_DH), const), pl.BlockSpec((1, DA_DH), const),
                  pl.BlockSpec((1, DA_DV), const)],
        out_specs=head_spec,
        out_shape=jax.ShapeDtypeStruct((B, H, S, DA_DV), bf16),
        scratch_shapes=[pltpu.VMEM((H, S, 2 * DA_DV), bf16),
                        *[pltpu.VMEM((2 * ATTN_ROWS, S), f32) for _ in range(SCORE_BUFFERS)],
                        *[pltpu.VMEM((2 * ATTN_ROWS, LANE), f32) for _ in range(SCORE_BUFFERS)]],
        compiler_params=pltpu.CompilerParams(
            dimension_semantics=("parallel",),
            vmem_limit_bytes=VMEM_LIMIT),
        name="diff_attn",
    )(q, k, v, sg, lq1, lk1, lq2, lk2, subln_g)


def _mix_out_kernel(res_ref, oda_ref, bg_ref, qm_ref, sgm_ref, u_ref, mem_ref, wkv_ref, cw_ref, wo_ref,
                    lng_ref, lnb_ref, y_ref, memk_ref, memv_ref, upad_ref):
    t = pl.program_id(1)
    rows = res_ref.shape[1]
    S = u_ref.shape[1]
    mxw = MX_HEADS * MX_DH

    @pl.when(t == 0)
    def _():
        kv = jnp.dot(mem_ref[0].astype(bf16), wkv_ref[...], preferred_element_type=f32)
        memk_ref[...] = kv[:, :mxw].astype(bf16)
        memv_ref[...] = kv[:, mxw:].astype(bf16)
        pad = jnp.zeros((CONV_PAD, u_ref.shape[2]), f32)
        upad_ref[0:CONV_PAD, :] = pad
        upad_ref[CONV_PAD + S:2 * CONV_PAD + S, :] = pad
        upad_ref[CONV_PAD:CONV_PAD + S, :] = u_ref[0]

    lane = lax.broadcasted_iota(jnp.int32, (OUT_SUB, mxw), 1)
    head_of_lane = lane // MX_DH

    def gated_conv(row0, bg_rows):
        w = upad_ref[pl.ds(pl.multiple_of(row0, CONV_ROWS), CONV_ROWS + 2 * CONV_PAD), :]
        shifted = [pltpu.roll(w, 1, 0), w, pltpu.roll(w, w.shape[0] - 1, 0)]
        taps = [shifted[k][CONV_PAD:CONV_PAD + CONV_ROWS, :] * cw_ref[k:k + 1, :] for k in range(3)]
        return (bg_ref[0, bg_rows, :].astype(f32) * (taps[0] + taps[1] + taps[2])).astype(bf16)

    def gated_conv_tile(s):
        return jnp.concatenate(
            [gated_conv(t * rows + s * OUT_SUB + c * CONV_ROWS, pl.ds(s * OUT_SUB + c * CONV_ROWS, CONV_ROWS))
             for c in range(OUT_SUB // CONV_ROWS)], axis=0)

    def gated_memory(s):
        sub = pl.ds(s * OUT_SUB, OUT_SUB)
        qm = qm_ref[0, sub, :]
        zero = jnp.zeros_like(qm)
        lhs = jnp.concatenate([jnp.where(head_of_lane == j, qm, zero) for j in range(MX_HEADS)], axis=0)
        sc = lax.dot_general(lhs, memk_ref[...], (((1,), (1,)), ((), ())), preferred_element_type=f32)
        m = jnp.max(sc, axis=-1, keepdims=True)
        p = jnp.exp2(sc - m)
        l = jnp.sum(p, axis=-1, keepdims=True)
        r = jnp.dot(p.astype(bf16), memv_ref[...], preferred_element_type=f32) / l
        o_mx = jnp.zeros((OUT_SUB, mxw), f32)
        for j in range(MX_HEADS):
            o_mx = jnp.where(head_of_lane == j, r[j * OUT_SUB:(j + 1) * OUT_SUB], o_mx)
        return (o_mx * sgm_ref[0, sub, :].astype(f32)).astype(bf16)

    def attention_part(s):
        oda = jnp.concatenate([oda_ref[0, j, pl.ds(s * OUT_SUB, OUT_SUB), :] for j in range(DA_HEADS)], axis=-1)
        return jnp.dot(oda, wo_ref[0:512, :], preferred_element_type=f32)

    subs = range(rows // OUT_SUB)
    attn = [attention_part(s) for s in subs]
    memory = [gated_memory(s) for s in subs]
    conv = [gated_conv_tile(s) for s in subs]
    mixed = list(zip(conv, memory))
    for s, (o_cv, o_mx) in enumerate(mixed):
        sub = pl.ds(s * OUT_SUB, OUT_SUB)
        out = (attn[s]
               + jnp.dot(o_cv, wo_ref[512:768, :], preferred_element_type=f32)
               + jnp.dot(o_mx, wo_ref[768:1024, :], preferred_element_type=f32))
        y_ref[0, sub, :] = _layer_norm(res_ref[0, sub, :] + out, lng_ref[...], lnb_ref[...])


def _mix_out(res, oda, bg, qm, sgm, u, mem, wkv, conv_w, wo, ln_g, ln_b):
    B, S, D = res.shape
    rows = OUT_ROWS
    n_mem = mem.shape[1]
    const = lambda b, t: (0, 0)
    row_spec = lambda w: pl.BlockSpec((1, rows, w), lambda b, t: (b, t, 0))
    return pl.pallas_call(
        _mix_out_kernel,
        grid=(B, S // rows),
        in_specs=[row_spec(D), pl.BlockSpec((1, DA_HEADS, rows, LANE), lambda b, t: (b, 0, t, 0)),
                  row_spec(256), row_spec(256), row_spec(256),
                  pl.BlockSpec((1, S, 256), lambda b, t: (b, 0, 0)),
                  pl.BlockSpec((1, n_mem, D), lambda b, t: (b, 0, 0)),
                  pl.BlockSpec(wkv.shape, const),
                  pl.BlockSpec(conv_w.shape, const),
                  pl.BlockSpec(wo.shape, const),
                  pl.BlockSpec((1, D), const), pl.BlockSpec((1, D), const)],
        out_specs=row_spec(D),
        out_shape=jax.ShapeDtypeStruct((B, S, D), f32),
        scratch_shapes=[pltpu.VMEM((n_mem, MX_HEADS * MX_DH), bf16),
                        pltpu.VMEM((n_mem, MX_HEADS * MX_DH), bf16),
                        pltpu.VMEM((S + 2 * CONV_PAD, 256), f32)],
        compiler_params=pltpu.CompilerParams(
            dimension_semantics=("parallel", "arbitrary"),
            vmem_limit_bytes=VMEM_LIMIT),
        name="mix_out",
    )(res, oda, bg, qm, sgm, u, mem, wkv, conv_w, wo, ln_g, ln_b)


def _rope_tables(seq_len):
    inv_freq = ROPE_THETA ** (-jnp.arange(0, ROT_DIM, 2, dtype=f32) / ROT_DIM)
    ang = jnp.arange(seq_len, dtype=f32)[:, None] * inv_freq[None, :]
    cos, sin = jnp.cos(ang), jnp.sin(ang)
    half = ROT_DIM // 2
    pad = DA_DH - ROT_DIM
    ones = jnp.ones((seq_len, pad), f32)
    zeros_h = jnp.zeros((seq_len, half), f32)
    zeros_p = jnp.zeros((seq_len, pad), f32)
    c64 = jnp.concatenate([cos, cos, ones], axis=-1)
    sp64 = jnp.concatenate([zeros_h, sin, zeros_p], axis=-1)
    sm64 = jnp.concatenate([-sin, zeros_h, zeros_p], axis=-1)
    tile = lambda a: jnp.concatenate([a, a], axis=-1)
    return tile(c64), tile(sp64), tile(sm64)


def _trunk(x, mem, in_g, in_b, w_in, wkv, lq1, lk1, lq2, lk2, subln_g, conv_w, wo, ln_g, ln_b, tables):
    q, k, v, sg, u, bg, qm, sgm, res = _in_proj(x, in_g, in_b, w_in, *tables)
    oda = _diff_attn(q, k, v, sg, lq1, lk1, lq2, lk2, subln_g)
    return _mix_out(res, oda, bg, qm, sgm, u, mem, wkv, conv_w, wo, ln_g, ln_b)


def kernel(x_prompt, x_sample, mem_prompt, mem_sample, in_ln_g, in_ln_b, w_in, w_mem_kv, lam_q1, lam_k1, lam_q2, lam_k2, subln_g, conv_w, w_o, ln_g, ln_b):
    assert w_in.shape[0] == 1, "single-layer trunk"
    assert x_prompt.shape[1] == x_sample.shape[1]
    tables = _rope_tables(x_prompt.shape[1])
    params = (in_ln_g[None, :], in_ln_b[None, :], w_in[0].astype(bf16), w_mem_kv[0].astype(bf16),
              lam_q1, lam_k1, lam_q2, lam_k2, subln_g, conv_w[0], w_o[0].astype(bf16), ln_g, ln_b)
    y_prompt = _trunk(x_prompt, mem_prompt, *params, tables)
    y_sample = _trunk(x_sample, mem_sample, *params, tables)
    return (y_prompt, y_sample)
```

```python
import math

import jax
import jax.numpy as jnp
from jax import lax
from jax.experimental import pallas as pl
from jax.experimental.pallas import tpu as pltpu

LANE = 128
DA_HEADS = 4
DA_DH = 64
DA_DV = 128
Q_SLOT, K_SLOT, V_SLOT, G_SLOT = 0, DA_HEADS, 2 * DA_HEADS, 3 * DA_HEADS
HEAD_SLOTS = 4 * DA_HEADS
MX_HEADS = 4
MX_DH = 64
ROT_DIM = 16
ROPE_THETA = 500000.0
LN_EPS = 1e-5
LAM_INIT = 0.8 - 0.6 * math.exp(-0.3 * 0)
ALPHA = 2.0 ** 0.25
LOG2E = math.log2(math.e)

PROJ_ROWS = 1024
PROJ_SUB = 512
ATTN_ROWS = 512
SCORE_BUFFERS = 3
STEPS_PER_TRIP = 6
FINISH_ROWS = 32
KEY_TILE = 256
OUT_ROWS = 1024
OUT_SUB = 512
CONV_ROWS = 64
CONV_PAD = 8
VMEM_LIMIT = 56 * 1024 * 1024

f32 = jnp.float32
bf16 = jnp.bfloat16


def _layer_norm(x, g, b):
    mu = jnp.mean(x, axis=-1, keepdims=True)
    xc = x - mu
    var = jnp.mean(xc * xc, axis=-1, keepdims=True)
    return xc * lax.rsqrt(var + LN_EPS) * g + b


def _silu(x):
    return x / (1.0 + jnp.exp(-x))


def _in_proj_kernel(x_ref, g_ref, b_ref, w_ref, cos_ref, sinp_ref, sinm_ref,
                    heads_ref, u_ref, gates_ref, res_ref):
    t = pl.program_id(1)
    rows = x_ref.shape[1]

    def rope(c, cos, sinp, sinm):
        return c * cos + pltpu.roll(c, 8, 1) * sinp + pltpu.roll(c, LANE - 8, 1) * sinm

    for s in range(rows // PROJ_SUB):
        sub = pl.ds(s * PROJ_SUB, PROJ_SUB)
        h32 = _layer_norm(x_ref[0, sub, :], g_ref[...], b_ref[...])
        res_ref[0, sub, :] = ALPHA * h32
        h = h32.astype(bf16)

        pos = pl.ds(pl.multiple_of(t * rows + s * PROJ_SUB, PROJ_SUB), PROJ_SUB)
        tabs = (cos_ref[pos, :], sinp_ref[pos, :], sinm_ref[pos, :])

        qk = jnp.dot(h, w_ref[:, 0:1024], preferred_element_type=f32)
        for j in range(DA_HEADS):
            heads_ref[0, Q_SLOT + j, sub, :] = (rope(qk[:, j * LANE:(j + 1) * LANE], *tabs) * (DA_DH ** -0.5 * LOG2E)).astype(bf16)
            heads_ref[0, K_SLOT + j, sub, :] = rope(qk[:, 512 + j * LANE:512 + (j + 1) * LANE], *tabs).astype(bf16)

        vg = jnp.dot(h, w_ref[:, 1024:2048], preferred_element_type=f32)
        for j in range(DA_HEADS):
            heads_ref[0, V_SLOT + j, sub, :] = vg[:, j * LANE:(j + 1) * LANE].astype(bf16)
            heads_ref[0, G_SLOT + j, sub, :] = _silu(vg[:, 512 + j * LANE:512 + (j + 1) * LANE]).astype(bf16)

        cv = jnp.dot(h, w_ref[:, 2048:3072], preferred_element_type=f32)
        u_ref[0, sub, :] = cv[:, 256:512] * cv[:, 512:768]
        gates_ref[0, sub, 0:256] = (cv[:, 0:256] * _silu(cv[:, 768:1024])).astype(bf16)

        mx = jnp.dot(h, w_ref[:, 3072:3584], preferred_element_type=f32)
        gates_ref[0, sub, 256:512] = (mx[:, 0:256] * (MX_DH ** -0.5 * LOG2E)).astype(bf16)
        gates_ref[0, sub, 512:768] = _silu(mx[:, 256:512]).astype(bf16)


def _in_proj(x, in_g, in_b, w_in, cos, sinp, sinm):
    B, S, D = x.shape
    rows = PROJ_ROWS
    grid = (B, S // rows)
    const = lambda b, t: (0, 0)
    resident = lambda shape: pl.BlockSpec(shape, const, pipeline_mode=pl.Buffered(1))
    row_spec = lambda w: pl.BlockSpec((1, rows, w), lambda b, t: (b, t, 0))
    return pl.pallas_call(
        _in_proj_kernel,
        grid=grid,
        in_specs=[
            pl.BlockSpec((1, rows, D), lambda b, t: (b, t, 0)),
            resident((1, D)), resident((1, D)), resident(w_in.shape),
            resident((S, LANE)), resident((S, LANE)), resident((S, LANE)),
        ],
        out_specs=[pl.BlockSpec((1, HEAD_SLOTS, rows, LANE), lambda b, t: (b, 0, t, 0)),
                   row_spec(256), row_spec(768), row_spec(D)],
        out_shape=[jax.ShapeDtypeStruct((B, HEAD_SLOTS, S, LANE), bf16),
                   jax.ShapeDtypeStruct((B, S, 256), f32),
                   jax.ShapeDtypeStruct((B, S, 768), bf16),
                   jax.ShapeDtypeStruct((B, S, D), f32)],
        compiler_params=pltpu.CompilerParams(
            dimension_semantics=("parallel", "arbitrary"),
            vmem_limit_bytes=VMEM_LIMIT),
        name="in_proj",
    )(x, in_g, in_b, w_in, cos, sinp, sinm)


def _diff_attn_kernel(heads_ref, lq1_ref, lk1_ref, lq2_ref, lk2_ref, sub_ref,
                      o_ref, vext_ref, *score_refs):
    bufs = list(zip(score_refs[:SCORE_BUFFERS], score_refs[SCORE_BUFFERS:]))
    H, S = o_ref.shape[1], o_ref.shape[2]
    tq = ATTN_ROWS
    tiles_per_head = S // tq
    n_tiles = H * tiles_per_head
    for h in range(H):
        vext_ref[h, :, 0:DA_DV] = heads_ref[0, V_SLOT + h]
        vext_ref[h, :, DA_DV:2 * DA_DV] = jnp.ones((S, DA_DV), bf16)

    lam = (jnp.exp(jnp.sum(lq1_ref[...] * lk1_ref[...], axis=-1, keepdims=True))
           - jnp.exp(jnp.sum(lq2_ref[...] * lk2_ref[...], axis=-1, keepdims=True)) + LAM_INIT)
    gain = sub_ref[...] * (1.0 - LAM_INIT)
    first_map = lax.broadcasted_iota(jnp.int32, (tq, LANE), 1) < DA_DH

    def tile_pos(n):
        return n // tiles_per_head, pl.multiple_of((n % tiles_per_head) * tq, tq)

    key_tiles = S // KEY_TILE

    def scores_and_row_max(n, s_ref, m_ref):
        h, r0 = tile_pos(n)
        qh = heads_ref[0, Q_SLOT + h, pl.ds(r0, tq), :]
        zero = jnp.zeros_like(qh)
        lhs = jnp.concatenate([jnp.where(first_map, qh, zero), jnp.where(first_map, zero, qh)], axis=0)
        m = None
        for j in range(key_tiles):
            keys = pl.ds(j * KEY_TILE, KEY_TILE)
            s = lax.dot_general(lhs, heads_ref[0, K_SLOT + h, keys, :], (((1,), (1,)), ((), ())), preferred_element_type=f32)
            s_ref[:, keys] = s
            for i in range(KEY_TILE // LANE):
                part = s[:, i * LANE:(i + 1) * LANE]
                m = part if m is None else jnp.maximum(m, part)
        m = jnp.max(m, axis=-1, keepdims=True)
        m_ref[...] = jnp.broadcast_to(m, (2 * tq, LANE))

    def weighted_values(n, s_ref, m_ref):
        h, r0 = tile_pos(n)
        pv = None
        for j in range(key_tiles):
            halves = [jnp.exp2(s_ref[:, pl.ds(j * KEY_TILE + i * LANE, LANE)] - m_ref[...]).astype(bf16)
                      for i in range(KEY_TILE // LANE)]
            part = jnp.dot(jnp.concatenate(halves, axis=1), vext_ref[h, pl.ds(j * KEY_TILE, KEY_TILE), :],
                           preferred_element_type=f32)
            pv = part if pv is None else pv + part
        for c in range(tq // FINISH_ROWS):
            a, b = c * FINISH_ROWS, (c + 1) * FINISH_ROWS
            o = pv[a:b, :DA_DV] / pv[a:b, DA_DV:] - lam * (pv[tq + a:tq + b, :DA_DV] / pv[tq + a:tq + b, DA_DV:])
            o = o * lax.rsqrt(jnp.mean(o * o, axis=-1, keepdims=True) + LN_EPS) * gain
            rows = pl.ds(r0 + a, FINISH_ROWS)
            o_ref[0, h, rows, :] = (o * heads_ref[0, G_SLOT + h, rows, :].astype(f32)).astype(bf16)

    scores_and_row_max(0, *bufs[0])

    def pipeline_steps(first, count):
        for d in range(count):
            scores_and_row_max(first + d, *bufs[(1 + d) % len(bufs)])
            weighted_values(first + d - 1, *bufs[d % len(bufs)])

    def steady(i, carry):
        pipeline_steps(STEPS_PER_TRIP * i + 1, STEPS_PER_TRIP)
        return carry

    assert STEPS_PER_TRIP % len(bufs) == 0 and (n_tiles - 1) % len(bufs) == 0
    trips = (n_tiles - 1) // STEPS_PER_TRIP
    lax.fori_loop(0, trips, steady, 0)
    pipeline_steps(trips * STEPS_PER_TRIP + 1, n_tiles - 1 - trips * STEPS_PER_TRIP)
    weighted_values(n_tiles - 1, *bufs[(n_tiles - 1) % len(bufs)])


def _diff_attn(heads, lq1, lk1, lq2, lk2, subln_g):
    B, _, S, _ = heads.shape
    H = DA_HEADS
    head_spec = pl.BlockSpec((1, H, S, LANE), lambda b: (b, 0, 0, 0))
    const = lambda b: (0, 0)
    return pl.pallas_call(
        _diff_attn_kernel,
        grid=(B,),
        in_specs=[pl.BlockSpec((1, HEAD_SLOTS, S, LANE), lambda b: (b, 0, 0, 0)),
                  pl.BlockSpec((1, DA_DH), const), pl.BlockSpec((1, DA_DH), const),
                  pl.BlockSpec((1, DA_DH), const), pl.BlockSpec((1, DA_DH), const),
                  pl.BlockSpec((1, DA_DV), const)],
        out_specs=head_spec,
        out_shape=jax.ShapeDtypeStruct((B, H, S, DA_DV), bf16),
        scratch_shapes=[pltpu.VMEM((H, S, 2 * DA_DV), bf16),
                        *[pltpu.VMEM((2 * ATTN_ROWS, S), f32) for _ in range(SCORE_BUFFERS)],
                        *[pltpu.VMEM((2 * ATTN_ROWS, LANE), f32) for _ in range(SCORE_BUFFERS)]],
        compiler_params=pltpu.CompilerParams(
            dimension_semantics=("parallel",),
            vmem_limit_bytes=VMEM_LIMIT),
        name="diff_attn",
    )(heads, lq1, lk1, lq2, lk2, subln_g)


def _mix_out_kernel(res_ref, oda_ref, gates_ref, u_ref, mem_ref, wkv_ref, cw_ref, wo_ref,
                    lng_ref, lnb_ref, y_ref, memk_ref, memv_ref, upad_ref):
    t = pl.program_id(1)
    rows = res_ref.shape[1]
    S = u_ref.shape[1]
    mxw = MX_HEADS * MX_DH

    @pl.when(t == 0)
    def _():
        kv = jnp.dot(mem_ref[0].astype(bf16), wkv_ref[...], preferred_element_type=f32)
        memk_ref[...] = kv[:, :mxw].astype(bf16)
        memv_ref[...] = kv[:, mxw:].astype(bf16)
        pad = jnp.zeros((CONV_PAD, u_ref.shape[2]), f32)
        upad_ref[0:CONV_PAD, :] = pad
        upad_ref[CONV_PAD + S:2 * CONV_PAD + S, :] = pad
        upad_ref[CONV_PAD:CONV_PAD + S, :] = u_ref[0]

    lane = lax.broadcasted_iota(jnp.int32, (OUT_SUB, mxw), 1)
    head_of_lane = lane // MX_DH

    def gated_conv(row0, bg_rows):
        w = upad_ref[pl.ds(pl.multiple_of(row0, CONV_ROWS), CONV_ROWS + 2 * CONV_PAD), :]
        shifted = [pltpu.roll(w, 1, 0), w, pltpu.roll(w, w.shape[0] - 1, 0)]
        taps = [shifted[k][CONV_PAD:CONV_PAD + CONV_ROWS, :] * cw_ref[k:k + 1, :] for k in range(3)]
        return (gates_ref[0, bg_rows, 0:256].astype(f32) * (taps[0] + taps[1] + taps[2])).astype(bf16)

    def gated_conv_tile(s):
        return jnp.concatenate(
            [gated_conv(t * rows + s * OUT_SUB + c * CONV_ROWS, pl.ds(s * OUT_SUB + c * CONV_ROWS, CONV_ROWS))
             for c in range(OUT_SUB // CONV_ROWS)], axis=0)

    def gated_memory(s):
        sub = pl.ds(s * OUT_SUB, OUT_SUB)
        qm = gates_ref[0, sub, 256:512]
        zero = jnp.zeros_like(qm)
        lhs = jnp.concatenate([jnp.where(head_of_lane == j, qm, zero) for j in range(MX_HEADS)], axis=0)
        sc = lax.dot_general(lhs, memk_ref[...], (((1,), (1,)), ((), ())), preferred_element_type=f32)
        m = jnp.max(sc, axis=-1, keepdims=True)
        p = jnp.exp2(sc - m)
        l = jnp.sum(p, axis=-1, keepdims=True)
        r = jnp.dot(p.astype(bf16), memv_ref[...], preferred_element_type=f32) / l
        o_mx = jnp.zeros((OUT_SUB, mxw), f32)
        for j in range(MX_HEADS):
            o_mx = jnp.where(head_of_lane == j, r[j * OUT_SUB:(j + 1) * OUT_SUB], o_mx)
        return (o_mx * gates_ref[0, sub, 512:768].astype(f32)).astype(bf16)

    def attention_part(s):
        oda = jnp.concatenate([oda_ref[0, j, pl.ds(s * OUT_SUB, OUT_SUB), :] for j in range(DA_HEADS)], axis=-1)
        return jnp.dot(oda, wo_ref[0:512, :], preferred_element_type=f32)

    subs = range(rows // OUT_SUB)
    attn = [attention_part(s) for s in subs]
    memory = [gated_memory(s) for s in subs]
    conv = [gated_conv_tile(s) for s in subs]
    mixed = list(zip(conv, memory))
    for s, (o_cv, o_mx) in enumerate(mixed):
        sub = pl.ds(s * OUT_SUB, OUT_SUB)
        out = (attn[s]
               + jnp.dot(o_cv, wo_ref[512:768, :], preferred_element_type=f32)
               + jnp.dot(o_mx, wo_ref[768:1024, :], preferred_element_type=f32))
        y_ref[0, sub, :] = _layer_norm(res_ref[0, sub, :] + out, lng_ref[...], lnb_ref[...])


def _mix_out(res, oda, gates, u, mem, wkv, conv_w, wo, ln_g, ln_b):
    B, S, D = res.shape
    rows = OUT_ROWS
    n_mem = mem.shape[1]
    const = lambda b, t: (0, 0)
    row_spec = lambda w: pl.BlockSpec((1, rows, w), lambda b, t: (b, t, 0))
    return pl.pallas_call(
        _mix_out_kernel,
        grid=(B, S // rows),
        in_specs=[row_spec(D), pl.BlockSpec((1, DA_HEADS, rows, LANE), lambda b, t: (b, 0, t, 0)),
                  row_spec(768),
                  pl.BlockSpec((1, S, 256), lambda b, t: (b, 0, 0)),
                  pl.BlockSpec((1, n_mem, D), lambda b, t: (b, 0, 0)),
                  pl.BlockSpec(wkv.shape, const),
                  pl.BlockSpec(conv_w.shape, const),
                  pl.BlockSpec(wo.shape, const),
                  pl.BlockSpec((1, D), const), pl.BlockSpec((1, D), const)],
        out_specs=row_spec(D),
        out_shape=jax.ShapeDtypeStruct((B, S, D), f32),
        scratch_shapes=[pltpu.VMEM((n_mem, MX_HEADS * MX_DH), bf16),
                        pltpu.VMEM((n_mem, MX_HEADS * MX_DH), bf16),
                        pltpu.VMEM((S + 2 * CONV_PAD, 256), f32)],
        compiler_params=pltpu.CompilerParams(
            dimension_semantics=("parallel", "arbitrary"),
            vmem_limit_bytes=VMEM_LIMIT),
        name="mix_out",
    )(res, oda, gates, u, mem, wkv, conv_w, wo, ln_g, ln_b)


def _rope_tables(seq_len):
    inv_freq = ROPE_THETA ** (-jnp.arange(0, ROT_DIM, 2, dtype=f32) / ROT_DIM)
    ang = jnp.arange(seq_len, dtype=f32)[:, None] * inv_freq[None, :]
    cos, sin = jnp.cos(ang), jnp.sin(ang)
    half = ROT_DIM // 2
    pad = DA_DH - ROT_DIM
    ones = jnp.ones((seq_len, pad), f32)
    zeros_h = jnp.zeros((seq_len, half), f32)
    zeros_p = jnp.zeros((seq_len, pad), f32)
    c64 = jnp.concatenate([cos, cos, ones], axis=-1)
    sp64 = jnp.concatenate([zeros_h, sin, zeros_p], axis=-1)
    sm64 = jnp.concatenate([-sin, zeros_h, zeros_p], axis=-1)
    tile = lambda a: jnp.concatenate([a, a], axis=-1)
    return tile(c64), tile(sp64), tile(sm64)


def _trunk(x, mem, in_g, in_b, w_in, wkv, lq1, lk1, lq2, lk2, subln_g, conv_w, wo, ln_g, ln_b, tables):
    heads, u, gates, res = _in_proj(x, in_g, in_b, w_in, *tables)
    oda = _diff_attn(heads, lq1, lk1, lq2, lk2, subln_g)
    return _mix_out(res, oda, gates, u, mem, wkv, conv_w, wo, ln_g, ln_b)


def kernel(x_prompt, x_sample, mem_prompt, mem_sample, in_ln_g, in_ln_b, w_in, w_mem_kv, lam_q1, lam_k1, lam_q2, lam_k2, subln_g, conv_w, w_o, ln_g, ln_b):
    assert w_in.shape[0] == 1, "single-layer trunk"
    assert x_prompt.shape[1] == x_sample.shape[1]
    tables = _rope_tables(x_prompt.shape[1])
    params = (in_ln_g[None, :], in_ln_b[None, :], w_in[0].astype(bf16), w_mem_kv[0].astype(bf16),
              lam_q1, lam_k1, lam_q2, lam_k2, subln_g, conv_w[0], w_o[0].astype(bf16), ln_g, ln_b)
    y_prompt = _trunk(x_prompt, mem_prompt, *params, tables)
    y_sample = _trunk(x_sample, mem_sample, *params, tables)
    return (y_prompt, y_sample)
```

```python
import math

import jax
import jax.numpy as jnp
from jax import lax
from jax.experimental import pallas as pl
from jax.experimental.pallas import tpu as pltpu

LANE = 128
DA_HEADS = 4
DA_DH = 64
DA_DV = 128
Q_SLOT, K_SLOT, V_SLOT, G_SLOT = 0, DA_HEADS, 2 * DA_HEADS, 3 * DA_HEADS
HEAD_SLOTS = 4 * DA_HEADS
MX_HEADS = 4
MX_DH = 64
ROT_DIM = 16
ROPE_THETA = 500000.0
LN_EPS = 1e-5
LAM_INIT = 0.8 - 0.6 * math.exp(-0.3 * 0)
ALPHA = 2.0 ** 0.25
LOG2E = math.log2(math.e)

PROJ_ROWS = 1024
PROJ_SUB = 512
ATTN_ROWS = 512
SCORE_BUFFERS = 3
STEPS_PER_TRIP = 6
FINISH_ROWS = 32
KEY_TILE = 256
OUT_ROWS = 1024
OUT_SUB = 512
CONV_ROWS = 64
CONV_PAD = 8
VMEM_LIMIT = 56 * 1024 * 1024

f32 = jnp.float32
bf16 = jnp.bfloat16


def _layer_norm(x, g, b):
    mu = jnp.mean(x, axis=-1, keepdims=True)
    xc = x - mu
    var = jnp.mean(xc * xc, axis=-1, keepdims=True)
    return xc * lax.rsqrt(var + LN_EPS) * g + b


def _silu(x):
    return x / (1.0 + jnp.exp(-x))


def _in_proj_kernel(x_ref, g_ref, b_ref, w_ref, cos_ref, sinp_ref, sinm_ref,
                    heads_ref, u_ref, gates_ref, res_ref):
    t = pl.program_id(1)
    rows = x_ref.shape[1]

    def rope(c, cos, sinp, sinm):
        return c * cos + pltpu.roll(c, 8, 1) * sinp + pltpu.roll(c, LANE - 8, 1) * sinm

    for s in range(rows // PROJ_SUB):
        sub = pl.ds(s * PROJ_SUB, PROJ_SUB)
        h32 = _layer_norm(x_ref[0, sub, :], g_ref[...], b_ref[...])
        res_ref[0, sub, :] = ALPHA * h32
        h = h32.astype(bf16)

        pos = pl.ds(pl.multiple_of(t * rows + s * PROJ_SUB, PROJ_SUB), PROJ_SUB)
        tabs = (cos_ref[pos, :], sinp_ref[pos, :], sinm_ref[pos, :])

        qk = jnp.dot(h, w_ref[:, 0:1024], preferred_element_type=f32)
        for j in range(DA_HEADS):
            heads_ref[0, Q_SLOT + j, sub, :] = (rope(qk[:, j * LANE:(j + 1) * LANE], *tabs) * (DA_DH ** -0.5 * LOG2E)).astype(bf16)
            heads_ref[0, K_SLOT + j, sub, :] = rope(qk[:, 512 + j * LANE:512 + (j + 1) * LANE], *tabs).astype(bf16)

        vg = jnp.dot(h, w_ref[:, 1024:2048], preferred_element_type=f32)
        for j in range(DA_HEADS):
            heads_ref[0, V_SLOT + j, sub, :] = vg[:, j * LANE:(j + 1) * LANE].astype(bf16)
            heads_ref[0, G_SLOT + j, sub, :] = _silu(vg[:, 512 + j * LANE:512 + (j + 1) * LANE]).astype(bf16)

        cv = jnp.dot(h, w_ref[:, 2048:3072], preferred_element_type=f32)
        u_ref[0, sub, :] = cv[:, 256:512] * cv[:, 512:768]
        gates_ref[0, sub, 0:256] = (cv[:, 0:256] * _silu(cv[:, 768:1024])).astype(bf16)

        mx = jnp.dot(h, w_ref[:, 3072:3584], preferred_element_type=f32)
        gates_ref[0, sub, 256:512] = (mx[:, 0:256] * (MX_DH ** -0.5 * LOG2E)).astype(bf16)
        gates_ref[0, sub, 512:768] = _silu(mx[:, 256:512]).astype(bf16)


def _in_proj(x, in_g, in_b, w_in, cos, sinp, sinm):
    B, S, D = x.shape
    rows = PROJ_ROWS
    grid = (B, S // rows)
    const = lambda b, t: (0, 0)
    resident = lambda shape: pl.BlockSpec(shape, const, pipeline_mode=pl.Buffered(1))
    row_spec = lambda w: pl.BlockSpec((1, rows, w), lambda b, t: (b, t, 0))
    return pl.pallas_call(
        _in_proj_kernel,
        grid=grid,
        in_specs=[
            pl.BlockSpec((1, rows, D), lambda b, t: (b, t, 0)),
            resident((1, D)), resident((1, D)), resident(w_in.shape),
            resident((S, LANE)), resident((S, LANE)), resident((S, LANE)),
        ],
        out_specs=[pl.BlockSpec((1, HEAD_SLOTS, rows, LANE), lambda b, t: (b, 0, t, 0)),
                   row_spec(256), row_spec(768), row_spec(D)],
        out_shape=[jax.ShapeDtypeStruct((B, HEAD_SLOTS, S, LANE), bf16),
                   jax.ShapeDtypeStruct((B, S, 256), f32),
                   jax.ShapeDtypeStruct((B, S, 768), bf16),
                   jax.ShapeDtypeStruct((B, S, D), f32)],
        compiler_params=pltpu.CompilerParams(
            dimension_semantics=("parallel", "arbitrary"),
            vmem_limit_bytes=VMEM_LIMIT),
        name="in_proj",
    )(x, in_g, in_b, w_in, cos, sinp, sinm)


def _diff_attn_kernel(heads_ref, lq1_ref, lk1_ref, lq2_ref, lk2_ref, sub_ref,
                      o_ref, vext_ref, *score_refs):
    bufs = list(zip(score_refs[:SCORE_BUFFERS], score_refs[SCORE_BUFFERS:]))
    H, S = o_ref.shape[1], o_ref.shape[2]
    tq = ATTN_ROWS
    tiles_per_head = S // tq
    n_tiles = H * tiles_per_head
    for h in range(H):
        vext_ref[h, :, 0:DA_DV] = heads_ref[0, V_SLOT + h]
        vext_ref[h, :, DA_DV:2 * DA_DV] = jnp.ones((S, DA_DV), bf16)

    lam = (jnp.exp(jnp.sum(lq1_ref[...] * lk1_ref[...], axis=-1, keepdims=True))
           - jnp.exp(jnp.sum(lq2_ref[...] * lk2_ref[...], axis=-1, keepdims=True)) + LAM_INIT)
    gain = sub_ref[...] * (1.0 - LAM_INIT)
    first_map = lax.broadcasted_iota(jnp.int32, (tq, LANE), 1) < DA_DH

    def tile_pos(n):
        return n // tiles_per_head, pl.multiple_of((n % tiles_per_head) * tq, tq)

    key_tiles = S // KEY_TILE

    def scores_and_row_max(n, s_ref, m_ref):
        h, r0 = tile_pos(n)
        qh = heads_ref[0, Q_SLOT + h, pl.ds(r0, tq), :]
        zero = jnp.zeros_like(qh)
        lhs = jnp.concatenate([jnp.where(first_map, qh, zero), jnp.where(first_map, zero, qh)], axis=0)
        m = None
        for j in range(key_tiles):
            keys = pl.ds(j * KEY_TILE, KEY_TILE)
            s = lax.dot_general(lhs, heads_ref[0, K_SLOT + h, keys, :], (((1,), (1,)), ((), ())), preferred_element_type=f32)
            s_ref[:, keys] = s
            for i in range(KEY_TILE // LANE):
                part = s[:, i * LANE:(i + 1) * LANE]
                m = part if m is None else jnp.maximum(m, part)
        m = jnp.max(m, axis=-1, keepdims=True)
        m_ref[...] = jnp.broadcast_to(m, (2 * tq, LANE))

    def weighted_values(n, s_ref, m_ref):
        h, r0 = tile_pos(n)
        pv = None
        for j in range(key_tiles):
            halves = [jnp.exp2(s_ref[:, pl.ds(j * KEY_TILE + i * LANE, LANE)] - m_ref[...]).astype(bf16)
                      for i in range(KEY_TILE // LANE)]
            part = jnp.dot(jnp.concatenate(halves, axis=1), vext_ref[h, pl.ds(j * KEY_TILE, KEY_TILE), :],
                           preferred_element_type=f32)
            pv = part if pv is None else pv + part
        for c in range(tq // FINISH_ROWS):
            a, b = c * FINISH_ROWS, (c + 1) * FINISH_ROWS
            o = pv[a:b, :DA_DV] / pv[a:b, DA_DV:] - lam * (pv[tq + a:tq + b, :DA_DV] / pv[tq + a:tq + b, DA_DV:])
            o = o * lax.rsqrt(jnp.mean(o * o, axis=-1, keepdims=True) + LN_EPS) * gain
            rows = pl.ds(r0 + a, FINISH_ROWS)
            o_ref[0, h, rows, :] = (o * heads_ref[0, G_SLOT + h, rows, :].astype(f32)).astype(bf16)

    scores_and_row_max(0, *bufs[0])

    def pipeline_steps(first, count):
        for d in range(count):
            scores_and_row_max(first + d, *bufs[(1 + d) % len(bufs)])
            weighted_values(first + d - 1, *bufs[d % len(bufs)])

    def steady(i, carry):
        pipeline_steps(STEPS_PER_TRIP * i + 1, STEPS_PER_TRIP)
        return carry

    assert STEPS_PER_TRIP % len(bufs) == 0 and (n_tiles - 1) % len(bufs) == 0
    trips = (n_tiles - 1) // STEPS_PER_TRIP
    lax.fori_loop(0, trips, steady, 0)
    pipeline_steps(trips * STEPS_PER_TRIP + 1, n_tiles - 1 - trips * STEPS_PER_TRIP)
    weighted_values(n_tiles - 1, *bufs[(n_tiles - 1) % len(bufs)])


def _diff_attn(heads, lq1, lk1, lq2, lk2, subln_g):
    B, _, S, _ = heads.shape
    H = DA_HEADS
    head_spec = pl.BlockSpec((1, H, S, LANE), lambda b: (b, 0, 0, 0))
    const = lambda b: (0, 0)
    return pl.pallas_call(
        _diff_attn_kernel,
        grid=(B,),
        in_specs=[pl.BlockSpec((1, HEAD_SLOTS, S, LANE), lambda b: (b, 0, 0, 0)),
                  pl.BlockSpec((1, DA_DH), const), pl.BlockSpec((1, DA_DH), const),
                  pl.BlockSpec((1, DA_DH), const), pl.BlockSpec((1, DA_DH), const),
                  pl.BlockSpec((1, DA_DV), const)],
        out_specs=head_spec,
        out_shape=jax.ShapeDtypeStruct((B, H, S, DA_DV), bf16),
        scratch_shapes=[pltpu.VMEM((H, S, 2 * DA_DV), bf16),
                        *[pltpu.VMEM((2 * ATTN_ROWS, S), f32) for _ in range(SCORE_BUFFERS)],
                        *[pltpu.VMEM((2 * ATTN_ROWS, LANE), f32) for _ in range(SCORE_BUFFERS)]],
        compiler_params=pltpu.CompilerParams(
            dimension_semantics=("parallel",),
            vmem_limit_bytes=VMEM_LIMIT),
        name="diff_attn",
    )(heads, lq1, lk1, lq2, lk2, subln_g)


def _mix_out_kernel(res_ref, oda_ref, gates_ref, u_ref, mem_ref, wkv_ref, cw_ref, wo_ref,
                    lng_ref, lnb_ref, y_ref, memk_ref, memv_ref, upad_ref):
    t = pl.program_id(1)
    rows = res_ref.shape[1]
    S = u_ref.shape[1]
    mxw = MX_HEADS * MX_DH

    @pl.when(t == 0)
    def _():
        kv = jnp.dot(mem_ref[0].astype(bf16), wkv_ref[...], preferred_element_type=f32)
        memk_ref[...] = kv[:, :mxw].astype(bf16)
        memv_ref[:, 0:mxw] = kv[:, mxw:].astype(bf16)
        memv_ref[:, mxw:2 * mxw] = jnp.ones((kv.shape[0], mxw), bf16)
        pad = jnp.zeros((CONV_PAD, u_ref.shape[2]), f32)
        upad_ref[0:CONV_PAD, :] = pad
        upad_ref[CONV_PAD + S:2 * CONV_PAD + S, :] = pad
        upad_ref[CONV_PAD:CONV_PAD + S, :] = u_ref[0]

    lane = lax.broadcasted_iota(jnp.int32, (OUT_SUB, mxw), 1)
    head_of_lane = lane // MX_DH

    def gated_conv(row0, bg_rows):
        w = upad_ref[pl.ds(pl.multiple_of(row0, CONV_ROWS), CONV_ROWS + 2 * CONV_PAD), :]
        shifted = [pltpu.roll(w, 1, 0), w, pltpu.roll(w, w.shape[0] - 1, 0)]
        taps = [shifted[k][CONV_PAD:CONV_PAD + CONV_ROWS, :] * cw_ref[k:k + 1, :] for k in range(3)]
        return (gates_ref[0, bg_rows, 0:256].astype(f32) * (taps[0] + taps[1] + taps[2])).astype(bf16)

    def gated_conv_tile(s):
        return jnp.concatenate(
            [gated_conv(t * rows + s * OUT_SUB + c * CONV_ROWS, pl.ds(s * OUT_SUB + c * CONV_ROWS, CONV_ROWS))
             for c in range(OUT_SUB // CONV_ROWS)], axis=0)

    def gated_memory(s):
        sub = pl.ds(s * OUT_SUB, OUT_SUB)
        qm = gates_ref[0, sub, 256:512]
        zero = jnp.zeros_like(qm)
        lhs = jnp.concatenate([jnp.where(head_of_lane == j, qm, zero) for j in range(MX_HEADS)], axis=0)
        sc = lax.dot_general(lhs, memk_ref[...], (((1,), (1,)), ((), ())), preferred_element_type=f32)
        m = jnp.max(sc, axis=-1, keepdims=True)
        p = jnp.exp2(sc - m)
        rl = jnp.dot(p.astype(bf16), memv_ref[...], preferred_element_type=f32)
        r = rl[:, 0:mxw] / rl[:, mxw:2 * mxw]
        o_mx = jnp.zeros((OUT_SUB, mxw), f32)
        for j in range(MX_HEADS):
            o_mx = jnp.where(head_of_lane == j, r[j * OUT_SUB:(j + 1) * OUT_SUB], o_mx)
        return (o_mx * gates_ref[0, sub, 512:768].astype(f32)).astype(bf16)

    def attention_part(s):
        oda = jnp.concatenate([oda_ref[0, j, pl.ds(s * OUT_SUB, OUT_SUB), :] for j in range(DA_HEADS)], axis=-1)
        return jnp.dot(oda, wo_ref[0:512, :], preferred_element_type=f32)

    subs = range(rows // OUT_SUB)
    attn = [attention_part(s) for s in subs]
    memory = [gated_memory(s) for s in subs]
    conv = [gated_conv_tile(s) for s in subs]
    mixed = list(zip(conv, memory))
    for s, (o_cv, o_mx) in enumerate(mixed):
        sub = pl.ds(s * OUT_SUB, OUT_SUB)
        out = (attn[s]
               + jnp.dot(o_cv, wo_ref[512:768, :], preferred_element_type=f32)
               + jnp.dot(o_mx, wo_ref[768:1024, :], preferred_element_type=f32))
        y_ref[0, sub, :] = _layer_norm(res_ref[0, sub, :] + out, lng_ref[...], lnb_ref[...])


def _mix_out(res, oda, gates, u, mem, wkv, conv_w, wo, ln_g, ln_b):
    B, S, D = res.shape
    rows = OUT_ROWS
    n_mem = mem.shape[1]
    const = lambda b, t: (0, 0)
    row_spec = lambda w: pl.BlockSpec((1, rows, w), lambda b, t: (b, t, 0))
    return pl.pallas_call(
        _mix_out_kernel,
        grid=(B, S // rows),
        in_specs=[row_spec(D), pl.BlockSpec((1, DA_HEADS, rows, LANE), lambda b, t: (b, 0, t, 0)),
                  row_spec(768),
                  pl.BlockSpec((1, S, 256), lambda b, t: (b, 0, 0)),
                  pl.BlockSpec((1, n_mem, D), lambda b, t: (b, 0, 0)),
                  pl.BlockSpec(wkv.shape, const),
                  pl.BlockSpec(conv_w.shape, const),
                  pl.BlockSpec(wo.shape, const),
                  pl.BlockSpec((1, D), const), pl.BlockSpec((1, D), const)],
        out_specs=row_spec(D),
        out_shape=jax.ShapeDtypeStruct((B, S, D), f32),
        scratch_shapes=[pltpu.VMEM((n_mem, MX_HEADS * MX_DH), bf16),
                        pltpu.VMEM((n_mem, 2 * MX_HEADS * MX_DH), bf16),
                        pltpu.VMEM((S + 2 * CONV_PAD, 256), f32)],
        compiler_params=pltpu.CompilerParams(
            dimension_semantics=("parallel", "arbitrary"),
            vmem_limit_bytes=VMEM_LIMIT),
        name="mix_out",
    )(res, oda, gates, u, mem, wkv, conv_w, wo, ln_g, ln_b)


def _rope_tables(seq_len):
    inv_freq = ROPE_THETA ** (-jnp.arange(0, ROT_DIM, 2, dtype=f32) / ROT_DIM)
    ang = jnp.arange(seq_len, dtype=f32)[:, None] * inv_freq[None, :]
    cos, sin = jnp.cos(ang), jnp.sin(ang)
    half = ROT_DIM // 2
    pad = DA_DH - ROT_DIM
    ones = jnp.ones((seq_len, pad), f32)
    zeros_h = jnp.zeros((seq_len, half), f32)
    zeros_p = jnp.zeros((seq_len, pad), f32)
    c64 = jnp.concatenate([cos, cos, ones], axis=-1)
    sp64 = jnp.concatenate([zeros_h, sin, zeros_p], axis=-1)
    sm64 = jnp.concatenate([-sin, zeros_h, zeros_p], axis=-1)
    tile = lambda a: jnp.concatenate([a, a], axis=-1)
    return tile(c64), tile(sp64), tile(sm64)


def _trunk(x, mem, in_g, in_b, w_in, wkv, lq1, lk1, lq2, lk2, subln_g, conv_w, wo, ln_g, ln_b, tables):
    heads, u, gates, res = _in_proj(x, in_g, in_b, w_in, *tables)
    oda = _diff_attn(heads, lq1, lk1, lq2, lk2, subln_g)
    return _mix_out(res, oda, gates, u, mem, wkv, conv_w, wo, ln_g, ln_b)


def kernel(x_prompt, x_sample, mem_prompt, mem_sample, in_ln_g, in_ln_b, w_in, w_mem_kv, lam_q1, lam_k1, lam_q2, lam_k2, subln_g, conv_w, w_o, ln_g, ln_b):
    assert w_in.shape[0] == 1, "single-layer trunk"
    assert x_prompt.shape[1] == x_sample.shape[1]
    tables = _rope_tables(x_prompt.shape[1])
    params = (in_ln_g[None, :], in_ln_b[None, :], w_in[0].astype(bf16), w_mem_kv[0].astype(bf16),
              lam_q1, lam_k1, lam_q2, lam_k2, subln_g, conv_w[0], w_o[0].astype(bf16), ln_g, ln_b)
    y_prompt = _trunk(x_prompt, mem_prompt, *params, tables)
    y_sample = _trunk(x_sample, mem_sample, *params, tables)
    return (y_prompt, y_sample)
```
